```python
import jax, jax.numpy as jnp
from jax import lax
import numpy as np

D_MODEL = 1024
BATCH = 4
SEQ = 4096
DEPTH = 1

D_LRU = D_MODEL // 2
D_CM = D_MODEL - D_LRU
D_MIX = D_LRU + D_CM
LRU_HEADS = 8
LRU_HEAD_DIM = D_LRU // LRU_HEADS
CM_GROUPS = 8
CONV_SHORT = 4
CONV_LONG = 31
LRU_C = 8.0
D_FF = ((8 * D_MODEL + 3 * 256 - 1) // (3 * 256)) * 256
D_IN = 2 * D_LRU + 2 * D_CM
EPS = 1e-6

kernel_name = "hymba_style_rglru_conformer_hybrid"


def rmsnorm(x, g):
    xf = x.astype(jnp.float32)
    y = xf * lax.rsqrt(jnp.mean(xf * xf, axis=-1, keepdims=True) + EPS)
    return (y * g.astype(jnp.float32)).astype(x.dtype)


def layernorm(x, g, b):
    xf = x.astype(jnp.float32)
    mu = jnp.mean(xf, axis=-1, keepdims=True)
    var = jnp.mean(jnp.square(xf - mu), axis=-1, keepdims=True)
    y = (xf - mu) * lax.rsqrt(var + EPS)
    return (y * g.astype(jnp.float32) + b.astype(jnp.float32)).astype(x.dtype)


def causal_depthwise_conv(x, w, b):
    k = w.shape[0]
    c = x.shape[-1]
    y = lax.conv_general_dilated(
        x, w[:, None, :].astype(x.dtype), window_strides=(1,), padding=[(k - 1, 0)],
        dimension_numbers=('NWC', 'WIO', 'NWC'), feature_group_count=c)
    return y + b.astype(x.dtype)


def _lru_combine(left, right):
    a_l, b_l = left
    a_r, b_r = right
    return a_l * a_r, a_r * b_l + b_r


def rg_lru(x, w_a, b_a, w_x, b_x, lam):
    bsz, s, _ = x.shape
    xh = x.reshape(bsz, s, LRU_HEADS, LRU_HEAD_DIM)
    r = jax.nn.sigmoid(jnp.einsum('bshi,hij->bshj', xh, w_a).reshape(bsz, s, D_LRU) + b_a)
    i = jax.nn.sigmoid(jnp.einsum('bshi,hij->bshj', xh, w_x).reshape(bsz, s, D_LRU) + b_x)
    log_a = -LRU_C * r.astype(jnp.float32) * jax.nn.softplus(-lam.astype(jnp.float32))
    a = jnp.exp(log_a)
    mult = jnp.sqrt(-jnp.expm1(2.0 * log_a))
    b_in = mult * (i * x).astype(jnp.float32)
    _, h = lax.associative_scan(_lru_combine, (a, b_in), axis=1)
    return h.astype(x.dtype)


def setup_inputs(seed: int = 0) -> dict:
    key = jax.random.key(seed)
    ks = jax.random.split(key, 24)
    f32 = jnp.float32

    def nrm(k, shape, fan_in, scale=1.0):
        return jax.random.normal(k, shape, f32) * (scale * fan_in ** -0.5)

    def gain(k, shape):
        return 1.0 + 0.05 * jax.random.normal(k, shape, f32)

    def bias(k, shape):
        return 0.02 * jax.random.normal(k, shape, f32)

    u = jax.random.uniform(ks[9], (DEPTH, D_LRU), f32, 0.9, 0.999)
    a0 = u ** (1.0 / LRU_C)
    lru_lambda = jnp.log(a0) - jnp.log1p(-a0)

    return {
        "x": jax.random.normal(ks[0], (BATCH, SEQ, D_MODEL), f32),
        "norm_mix": gain(ks[1], (DEPTH, D_MODEL)),
        "w_in": nrm(ks[2], (DEPTH, D_MODEL, D_IN), D_MODEL),
        "conv4_w": nrm(ks[3], (DEPTH, CONV_SHORT, D_LRU), CONV_SHORT),
        "conv4_b": bias(ks[4], (DEPTH, D_LRU)),
        "gate_a_w": nrm(ks[5], (DEPTH, LRU_HEADS, LRU_HEAD_DIM, LRU_HEAD_DIM), LRU_HEAD_DIM),
        "gate_a_b": bias(ks[6], (DEPTH, D_LRU)),
        "gate_x_w": nrm(ks[7], (DEPTH, LRU_HEADS, LRU_HEAD_DIM, LRU_HEAD_DIM), LRU_HEAD_DIM),
        "gate_x_b": bias(ks[8], (DEPTH, D_LRU)),
        "lru_lambda": lru_lambda,
        "dw31_w": nrm(ks[10], (DEPTH, CONV_LONG, D_CM), CONV_LONG),
        "dw31_b": bias(ks[11], (DEPTH, D_CM)),
        "cm_ln_g": gain(ks[12], (DEPTH, D_CM)),
        "cm_ln_b": bias(ks[13], (DEPTH, D_CM)),
        "out_norm_lru": gain(ks[14], (DEPTH, D_LRU)),
        "out_norm_cm": gain(ks[15], (DEPTH, D_CM)),
        "w_out": nrm(ks[16], (DEPTH, D_MIX, D_MODEL), D_MIX, 0.5),
        "norm_ffn": gain(ks[17], (DEPTH, D_MODEL)),
        "w_gate": nrm(ks[18], (DEPTH, D_MODEL, D_FF), D_MODEL),
        "w_up": nrm(ks[19], (DEPTH, D_MODEL, D_FF), D_MODEL),
        "w_down": nrm(ks[20], (DEPTH, D_FF, D_MODEL), D_FF, 0.5),
        "norm_final": gain(ks[21], (D_MODEL,)),
    }


def reference(x, norm_mix, w_in, conv4_w, conv4_b, gate_a_w, gate_a_b, gate_x_w, gate_x_b,
              lru_lambda, dw31_w, dw31_b, cm_ln_g, cm_ln_b, out_norm_lru, out_norm_cm,
              w_out, norm_ffn, w_gate, w_up, w_down, norm_final):
    for l in range(DEPTH):
        h = rmsnorm(x, norm_mix[l])
        proj = jnp.einsum('bsd,de->bse', h, w_in[l])
        x_lru = proj[..., :D_LRU]
        g_lru = proj[..., D_LRU:2 * D_LRU]
        v_cm = proj[..., 2 * D_LRU:2 * D_LRU + D_CM]
        g_cm = proj[..., 2 * D_LRU + D_CM:]

        c = causal_depthwise_conv(x_lru, conv4_w[l], conv4_b[l])
        y_lru = rg_lru(c, gate_a_w[l], gate_a_b[l], gate_x_w[l], gate_x_b[l], lru_lambda[l])
        y_lru = y_lru * jax.nn.gelu(g_lru)

        glu = v_cm * jax.nn.sigmoid(g_cm)
        d = causal_depthwise_conv(glu, dw31_w[l], dw31_b[l])
        y_cm = jax.nn.silu(layernorm(d, cm_ln_g[l], cm_ln_b[l]))

        y = jnp.concatenate([rmsnorm(y_lru, out_norm_lru[l]),
                             rmsnorm(y_cm, out_norm_cm[l])], axis=-1)
        x = x + jnp.einsum('bse,ed->bsd', y, w_out[l])

        h = rmsnorm(x, norm_ffn[l])
        f = jax.nn.silu(jnp.einsum('bsd,df->bsf', h, w_gate[l])) * jnp.einsum('bsd,df->bsf', h, w_up[l])
        x = x + jnp.einsum('bsf,fd->bsd', f, w_down[l])
    return rmsnorm(x, norm_final)
```

```python
import functools
import math

import jax
import jax.numpy as jnp
from jax import lax
from jax.experimental import pallas as pl
from jax.experimental.pallas import tpu as pltpu

EPS = 1e-6
LRU_C = 8.0
SUBLANES = 8
CONV_SHORT = 4
CONV_LONG = 31
HALO_LONG = 32
SEQ_TILE = 512
TOK_TILE = 512
ROW_CHUNK = 32
VMEM_LIMIT_BYTES = 56 * 1024 * 1024


def _sigmoid(v):
    return 1.0 / (1.0 + jnp.exp(-v))


def _gelu_tanh(v):
    inner = math.sqrt(2.0 / math.pi) * (v + 0.044715 * (v * v * v))
    return v * (0.5 * (1.0 + jnp.tanh(inner)))


def _rms_scale(v, width):
    return lax.rsqrt(jnp.sum(v * v, axis=-1, keepdims=True) * (1.0 / width) + EPS)


def _row_chunks(n_rows, body):
    def step(ci, carry):
        r0 = pl.multiple_of(ci * ROW_CHUNK, ROW_CHUNK)
        return body(r0, carry)
    return step, n_rows // ROW_CHUNK


def _mixer_kernel(x_ref, nmix_ref, w_in_ref, c4w_ref, c4b_ref, wgate_ref, gab_ref, gxb_ref,
                  lam_ref, dww_ref, dwb_ref, lng_ref, lnb_ref, onl_ref, onc_ref, w_out_ref,
                  o_ref,
                  h_s, proj_s, c_s, cb_s, gates_s, ext_s, sh_s, y_s, hc_s):
    ts = x_ref.shape[1]
    d_model = x_ref.shape[2]
    d_lru = c4b_ref.shape[1]
    d_cm = dwb_ref.shape[1]
    rc = ROW_CHUNK
    j = pl.program_id(1)

    @pl.when(j == 0)
    def _():
        proj_s[0:SUBLANES, 0:d_lru] = jnp.zeros((SUBLANES, d_lru), jnp.float32)
        ext_s[0:HALO_LONG, :] = jnp.zeros((HALO_LONG, d_cm), jnp.float32)
        hc_s[...] = jnp.zeros_like(hc_s)

    nmix = nmix_ref[...]

    def norm_body(r0, carry):
        xv = x_ref[0, pl.ds(r0, rc), :]
        h_s[pl.ds(r0, rc), :] = (xv * _rms_scale(xv, d_model) * nmix).astype(jnp.bfloat16)
        return carry
    step, n = _row_chunks(ts, norm_body)
    lax.fori_loop(0, n, step, 0)

    proj_s[pl.ds(SUBLANES, ts), :] = jnp.dot(h_s[...], w_in_ref[...],
                                             preferred_element_type=jnp.float32)

    c4w = c4w_ref[...]
    c4b = c4b_ref[...]

    def pre_body(r0, carry):
        win = proj_s[pl.ds(r0, rc + SUBLANES), 0:d_lru]
        acc = c4b + c4w[CONV_SHORT - 1:CONV_SHORT, :] * win[SUBLANES:, :]
        for s in range(1, CONV_SHORT):
            tap = pltpu.roll(win, s, axis=0)[SUBLANES:, :]
            acc = acc + c4w[CONV_SHORT - 1 - s:CONV_SHORT - s, :] * tap
        c_s[pl.ds(r0, rc), :] = acc
        cb_s[pl.ds(r0, rc), :] = acc.astype(jnp.bfloat16)
        v = proj_s[pl.ds(r0 + SUBLANES, rc), 2 * d_lru:2 * d_lru + d_cm]
        g = proj_s[pl.ds(r0 + SUBLANES, rc), 2 * d_lru + d_cm:]
        ext_s[pl.ds(r0 + HALO_LONG, rc), :] = v * _sigmoid(g)
        return carry
    step, n = _row_chunks(ts, pre_body)
    lax.fori_loop(0, n, step, 0)

    def shift_rows(r0, n_rows):
        win = ext_s[pl.ds(r0, n_rows + SUBLANES), :]
        for r in range(1, SUBLANES):
            rolled = pltpu.roll(win, n_rows + SUBLANES - r, axis=0)
            sh_s[r - 1, pl.ds(r0, n_rows), :] = rolled[0:n_rows, :]

    def shift_body(r0, carry):
        shift_rows(r0, rc)
        return carry
    step, n = _row_chunks(ts, shift_body)
    lax.fori_loop(0, n, step, 0)
    shift_rows(ts, HALO_LONG - SUBLANES)

    gates_s[...] = jnp.dot(cb_s[...], wgate_ref[...], preferred_element_type=jnp.float32)

    gab = gab_ref[...]
    gxb = gxb_ref[...]
    z = -lam_ref[...]
    neg_c_softplus = -LRU_C * (jnp.maximum(z, 0.0) + jnp.log1p(jnp.exp(-jnp.abs(z))))
    onl = onl_ref[...]
    groups = rc // SUBLANES
    row_in_group = lax.broadcasted_iota(jnp.int32, (groups, SUBLANES, d_lru), 1)

    def lru_body(r0, hprev):
        cv = c_s[pl.ds(r0, rc), :]
        r_gate = _sigmoid(gates_s[pl.ds(r0, rc), 0:d_lru] + gab)
        i_gate = _sigmoid(gates_s[pl.ds(r0, rc), d_lru:] + gxb)
        log_a = neg_c_softplus * r_gate
        a = jnp.exp(log_a)
        b = jnp.sqrt(-jnp.tanh(log_a) * (a * a + 1.0)) * (i_gate * cv)
        a3 = a.reshape(groups, SUBLANES, d_lru)
        b3 = b.reshape(groups, SUBLANES, d_lru)
        for s in (1, 2, 4):
            keep = row_in_group >= s
            a_prev = jnp.where(keep, pltpu.roll(a3, s, axis=1), 1.0)
            b_prev = jnp.where(keep, pltpu.roll(b3, s, axis=1), 0.0)
            b3 = a3 * b_prev + b3
            a3 = a3 * a_prev
        hs = []
        for gi in range(groups):
            hg = b3[gi] + a3[gi] * hprev
            hs.append(hg)
            hprev = jnp.broadcast_to(hg[SUBLANES - 1:SUBLANES, :], (SUBLANES, d_lru))
        hv = jnp.concatenate(hs, axis=0)
        y = hv * _gelu_tanh(proj_s[pl.ds(r0 + SUBLANES, rc), d_lru:2 * d_lru])
        y_s[pl.ds(r0, rc), 0:d_lru] = (y * _rms_scale(y, d_lru) * onl).astype(jnp.bfloat16)
        return hprev
    step, n = _row_chunks(ts, lru_body)
    hc_s[...] = lax.fori_loop(0, n, step, hc_s[...])

    dwb = dwb_ref[...]
    lng = lng_ref[...]
    lnb = lnb_ref[...]
    onc = onc_ref[...]
    first_tap = HALO_LONG - (CONV_LONG - 1)

    def conv_body(r0, carry):
        acc = jnp.broadcast_to(dwb, (rc, d_cm)).reshape(groups, SUBLANES, d_cm)
        for k in range(CONV_LONG):
            q, r = divmod(first_tap + k, SUBLANES)
            src = ext_s if r == 0 else sh_s.at[r - 1]
            tap = src[pl.ds(r0 + q * SUBLANES, rc), :].reshape(groups, SUBLANES, d_cm)
            wk = dww_ref[k * SUBLANES:(k + 1) * SUBLANES, :]
            acc = acc + wk[None] * tap
        d = acc.reshape(rc, d_cm)
        mu = jnp.sum(d, axis=-1, keepdims=True) * (1.0 / d_cm)
        dc = d - mu
        var = jnp.sum(dc * dc, axis=-1, keepdims=True) * (1.0 / d_cm)
        ln = dc * lax.rsqrt(var + EPS) * lng + lnb
        y = ln * _sigmoid(ln)
        y_s[pl.ds(r0, rc), d_lru:] = (y * _rms_scale(y, d_cm) * onc).astype(jnp.bfloat16)
        return carry
    step, n = _row_chunks(ts, conv_body)
    lax.fori_loop(0, n, step, 0)

    o_ref[0] = x_ref[0] + jnp.dot(y_s[...], w_out_ref[...], preferred_element_type=jnp.float32)

    proj_s[0:SUBLANES, 0:d_lru] = proj_s[ts:ts + SUBLANES, 0:d_lru]
    ext_s[0:HALO_LONG, :] = ext_s[ts:ts + HALO_LONG, :]


def _ffn_kernel(x_ref, nffn_ref, wg_ref, wu_ref, wd_ref, nfin_ref, o_ref, h_s, g_s, u_s, f_s,
                *, final_norm):
    tm, d_model = x_ref.shape
    rc = ROW_CHUNK
    nffn = nffn_ref[...]
    nfin = nfin_ref[...]

    def norm_body(r0, carry):
        xv = x_ref[pl.ds(r0, rc), :]
        h_s[pl.ds(r0, rc), :] = (xv * _rms_scale(xv, d_model) * nffn).astype(jnp.bfloat16)
        return carry
    step, n = _row_chunks(tm, norm_body)
    lax.fori_loop(0, n, step, 0)

    g_s[...] = jnp.dot(h_s[...], wg_ref[...], preferred_element_type=jnp.float32)
    u_s[...] = jnp.dot(h_s[...], wu_ref[...], preferred_element_type=jnp.float32)

    def act_body(r0, carry):
        g = g_s[pl.ds(r0, rc), :]
        f_s[pl.ds(r0, rc), :] = (g * _sigmoid(g) * u_s[pl.ds(r0, rc), :]).astype(jnp.bfloat16)
        return carry
    step, n = _row_chunks(tm, act_body)
    lax.fori_loop(0, n, step, 0)

    o_ref[...] = x_ref[...] + jnp.dot(f_s[...], wd_ref[...], preferred_element_type=jnp.float32)

    def final_body(r0, carry):
        v = o_ref[pl.ds(r0, rc), :]
        o_ref[pl.ds(r0, rc), :] = v * _rms_scale(v, d_model) * nfin
        return carry
    if final_norm:
        step, n = _row_chunks(tm, final_body)
        lax.fori_loop(0, n, step, 0)


def _resident(shape):
    zeros = (0,) * len(shape)
    return pl.BlockSpec(shape, lambda *_: zeros, pipeline_mode=pl.Buffered(1))


def _block_diag(w):
    heads, d, _ = w.shape
    eye = jnp.eye(heads, dtype=w.dtype)
    return jnp.einsum('hij,hg->higj', w, eye).reshape(heads * d, heads * d)


def _mixer_layer(x, norm_mix, w_in, conv4_w, conv4_b, gate_a_w, gate_a_b, gate_x_w, gate_x_b,
                 lru_lambda, dw31_w, dw31_b, cm_ln_g, cm_ln_b, out_norm_lru, out_norm_cm, w_out):
    bsz, seq, d_model = x.shape
    d_lru = conv4_b.shape[0]
    d_cm = dw31_b.shape[0]
    ts = SEQ_TILE
    assert seq % ts == 0 and ts % ROW_CHUNK == 0
    assert conv4_w.shape[0] == CONV_SHORT and dw31_w.shape[0] == CONV_LONG
    bf16 = jnp.bfloat16
    row = lambda v: v.reshape(1, -1)
    w_gate = jnp.concatenate([_block_diag(gate_a_w), _block_diag(gate_x_w)], axis=1).astype(bf16)
    dww_rows = jnp.repeat(dw31_w, SUBLANES, axis=0)
    operands = (
        x, row(norm_mix), w_in.astype(bf16), conv4_w, row(conv4_b), w_gate, row(gate_a_b),
        row(gate_x_b), row(lru_lambda), dww_rows, row(dw31_b), row(cm_ln_g), row(cm_ln_b),
        row(out_norm_lru), row(out_norm_cm), w_out.astype(bf16))
    in_specs = [pl.BlockSpec((1, ts, d_model), lambda b, j: (b, j, 0))]
    in_specs += [_resident(op.shape) for op in operands[1:]]
    f32 = jnp.float32
    scratch = [
        pltpu.VMEM((ts, d_model), bf16),
        pltpu.VMEM((ts + SUBLANES, w_in.shape[1]), f32),
        pltpu.VMEM((ts, d_lru), f32),
        pltpu.VMEM((ts, d_lru), bf16),
        pltpu.VMEM((ts, 2 * d_lru), f32),
        pltpu.VMEM((ts + HALO_LONG, d_cm), f32),
        pltpu.VMEM((SUBLANES - 1, ts + HALO_LONG - SUBLANES, d_cm), f32),
        pltpu.VMEM((ts, d_lru + d_cm), bf16),
        pltpu.VMEM((SUBLANES, d_lru), f32),
    ]
    return pl.pallas_call(
        _mixer_kernel,
        grid=(bsz, seq // ts),
        in_specs=in_specs,
        out_specs=pl.BlockSpec((1, ts, d_model), lambda b, j: (b, j, 0)),
        out_shape=jax.ShapeDtypeStruct(x.shape, x.dtype),
        scratch_shapes=scratch,
        compiler_params=pltpu.CompilerParams(
            dimension_semantics=("arbitrary", "arbitrary"),
            vmem_limit_bytes=VMEM_LIMIT_BYTES),
        name="mixer",
    )(*operands)


def _ffn_layer(x, norm_ffn, w_gate, w_up, w_down, norm_final, final_norm):
    bsz, seq, d_model = x.shape
    d_ff = w_gate.shape[1]
    n_tok = bsz * seq
    tm = TOK_TILE
    assert n_tok % tm == 0 and tm % ROW_CHUNK == 0
    bf16 = jnp.bfloat16
    operands = (x.reshape(n_tok, d_model), norm_ffn.reshape(1, -1), w_gate.astype(bf16),
                w_up.astype(bf16), w_down.astype(bf16), norm_final.reshape(1, -1))
    in_specs = [pl.BlockSpec((tm, d_model), lambda i: (i, 0))]
    in_specs += [_resident(op.shape) for op in operands[1:]]
    out = pl.pallas_call(
        functools.partial(_ffn_kernel, final_norm=final_norm),
        grid=(n_tok // tm,),
        in_specs=in_specs,
        out_specs=pl.BlockSpec((tm, d_model), lambda i: (i, 0)),
        out_shape=jax.ShapeDtypeStruct((n_tok, d_model), x.dtype),
        scratch_shapes=[
            pltpu.VMEM((tm, d_model), bf16),
            pltpu.VMEM((tm, d_ff), jnp.float32),
            pltpu.VMEM((tm, d_ff), jnp.float32),
            pltpu.VMEM((tm, d_ff), bf16),
        ],
        compiler_params=pltpu.CompilerParams(
            dimension_semantics=("arbitrary",),
            vmem_limit_bytes=VMEM_LIMIT_BYTES),
        name="ffn",
    )(*operands)
    return out.reshape(bsz, seq, d_model)


def kernel(x, norm_mix, w_in, conv4_w, conv4_b, gate_a_w, gate_a_b, gate_x_w, gate_x_b, lru_lambda, dw31_w, dw31_b, cm_ln_g, cm_ln_b, out_norm_lru, out_norm_cm, w_out, norm_ffn, w_gate, w_up, w_down, norm_final):
    depth = w_in.shape[0]
    for l in range(depth):
        x = _mixer_layer(x, norm_mix[l], w_in[l], conv4_w[l], conv4_b[l], gate_a_w[l], gate_a_b[l],
                         gate_x_w[l], gate_x_b[l], lru_lambda[l], dw31_w[l], dw31_b[l], cm_ln_g[l],
                         cm_ln_b[l], out_norm_lru[l], out_norm_cm[l], w_out[l])
        x = _ffn_layer(x, norm_ffn[l], w_gate[l], w_up[l], w_down[l], norm_final,
                       final_norm=(l == depth - 1))
    return x
```

```python
import functools
import math

import jax
import jax.numpy as jnp
from jax import lax
from jax.experimental import pallas as pl
from jax.experimental.pallas import tpu as pltpu

EPS = 1e-6
LRU_C = 8.0
SUBLANES = 8
CONV_SHORT = 4
CONV_LONG = 31
HALO_LONG = 32
SEQ_TILE = 512
TOK_TILE = 512
ROW_CHUNK = 32
FFN_SUB_ROWS = 256
FFN_COL_CHUNK = 256
VMEM_LIMIT_BYTES = 56 * 1024 * 1024


def _sigmoid(v):
    return 0.5 + 0.5 * jnp.tanh(0.5 * v)


def _silu(v):
    hv = 0.5 * v
    return hv + hv * jnp.tanh(hv)


def _gelu_tanh(v):
    inner = math.sqrt(2.0 / math.pi) * (v + 0.044715 * (v * v * v))
    return v * (0.5 * (1.0 + jnp.tanh(inner)))


def _rms_scale(v, width):
    return lax.rsqrt(jnp.sum(v * v, axis=-1, keepdims=True) * (1.0 / width) + EPS)


def _row_chunks(n_rows, body):
    def step(ci, carry):
        r0 = pl.multiple_of(ci * ROW_CHUNK, ROW_CHUNK)
        return body(r0, carry)
    return step, n_rows // ROW_CHUNK


def _mixer_kernel(x_ref, nmix_ref, w_in_ref, c4w_ref, c4b_ref, wgate_ref, gab_ref, gxb_ref,
                  lam_ref, dww_ref, dwb_ref, lng_ref, lnb_ref, onl_ref, onc_ref, w_out_ref,
                  o_ref,
                  h_s, proj_s, c_s, cb_s, gates_s, ext_s, sh_s, y_s, hc_s):
    ts = x_ref.shape[1]
    d_model = x_ref.shape[2]
    d_lru = c4b_ref.shape[1]
    d_cm = dwb_ref.shape[1]
    rc = ROW_CHUNK
    j = pl.program_id(1)

    @pl.when(j == 0)
    def _():
        proj_s[0:SUBLANES, 0:d_lru] = jnp.zeros((SUBLANES, d_lru), jnp.float32)
        ext_s[0:HALO_LONG, :] = jnp.zeros((HALO_LONG, d_cm), jnp.float32)
        hc_s[...] = jnp.zeros_like(hc_s)

    nmix = nmix_ref[...]

    def norm_body(r0, carry):
        xv = x_ref[0, pl.ds(r0, rc), :]
        h_s[pl.ds(r0, rc), :] = (xv * _rms_scale(xv, d_model) * nmix).astype(jnp.bfloat16)
        return carry
    step, n = _row_chunks(ts, norm_body)
    lax.fori_loop(0, n, step, 0)

    proj_s[pl.ds(SUBLANES, ts), :] = jnp.dot(h_s[...], w_in_ref[...],
                                             preferred_element_type=jnp.float32)

    c4w = c4w_ref[...]
    c4b = c4b_ref[...]

    def pre_body(r0, carry):
        win = proj_s[pl.ds(r0, rc + SUBLANES), 0:d_lru]
        acc = c4b + c4w[CONV_SHORT - 1:CONV_SHORT, :] * win[SUBLANES:, :]
        for s in range(1, CONV_SHORT):
            tap = pltpu.roll(win, s, axis=0)[SUBLANES:, :]
            acc = acc + c4w[CONV_SHORT - 1 - s:CONV_SHORT - s, :] * tap
        c_s[pl.ds(r0, rc), :] = acc
        cb_s[pl.ds(r0, rc), :] = acc.astype(jnp.bfloat16)
        v = proj_s[pl.ds(r0 + SUBLANES, rc), 2 * d_lru:2 * d_lru + d_cm]
        g = proj_s[pl.ds(r0 + SUBLANES, rc), 2 * d_lru + d_cm:]
        ext_s[pl.ds(r0 + HALO_LONG, rc), :] = v * _sigmoid(g)
        return carry
    step, n = _row_chunks(ts, pre_body)
    lax.fori_loop(0, n, step, 0)

    def shift_rows(r0, n_rows):
        win = ext_s[pl.ds(r0, n_rows + SUBLANES), :]
        for r in range(1, SUBLANES):
            rolled = pltpu.roll(win, n_rows + SUBLANES - r, axis=0)
            sh_s[r - 1, pl.ds(r0, n_rows), :] = rolled[0:n_rows, :]

    def shift_body(r0, carry):
        shift_rows(r0, rc)
        return carry
    step, n = _row_chunks(ts, shift_body)
    lax.fori_loop(0, n, step, 0)
    shift_rows(ts, HALO_LONG - SUBLANES)

    gates_s[...] = jnp.dot(cb_s[...], wgate_ref[...], preferred_element_type=jnp.float32)

    gab = gab_ref[...]
    gxb = gxb_ref[...]
    z = -lam_ref[...]
    neg_c_softplus = -LRU_C * (jnp.maximum(z, 0.0) + jnp.log1p(jnp.exp(-jnp.abs(z))))
    onl = onl_ref[...]
    groups = rc // SUBLANES
    row_in_group = lax.broadcasted_iota(jnp.int32, (groups, SUBLANES, d_lru), 1)

    def lru_body(r0, hprev):
        cv = c_s[pl.ds(r0, rc), :]
        r_gate = _sigmoid(gates_s[pl.ds(r0, rc), 0:d_lru] + gab)
        i_gate = _sigmoid(gates_s[pl.ds(r0, rc), d_lru:] + gxb)
        log_a = neg_c_softplus * r_gate
        a = jnp.exp(log_a)
        b = jnp.sqrt(-jnp.tanh(log_a) * (a * a + 1.0)) * (i_gate * cv)
        a3 = a.reshape(groups, SUBLANES, d_lru)
        b3 = b.reshape(groups, SUBLANES, d_lru)
        for s in (1, 2, 4):
            keep = row_in_group >= s
            a_prev = jnp.where(keep, pltpu.roll(a3, s, axis=1), 1.0)
            b_prev = jnp.where(keep, pltpu.roll(b3, s, axis=1), 0.0)
            b3 = a3 * b_prev + b3
            a3 = a3 * a_prev
        hs = []
        for gi in range(groups):
            hg = b3[gi] + a3[gi] * hprev
            hs.append(hg)
            hprev = jnp.broadcast_to(hg[SUBLANES - 1:SUBLANES, :], (SUBLANES, d_lru))
        hv = jnp.concatenate(hs, axis=0)
        y = hv * _gelu_tanh(proj_s[pl.ds(r0 + SUBLANES, rc), d_lru:2 * d_lru])
        y_s[pl.ds(r0, rc), 0:d_lru] = (y * _rms_scale(y, d_lru) * onl).astype(jnp.bfloat16)
        return hprev
    step, n = _row_chunks(ts, lru_body)
    hc_s[...] = lax.fori_loop(0, n, step, hc_s[...])

    dwb = dwb_ref[...]
    lng = lng_ref[...]
    lnb = lnb_ref[...]
    onc = onc_ref[...]
    first_tap = HALO_LONG - (CONV_LONG - 1)

    def conv_body(r0, carry):
        acc = jnp.broadcast_to(dwb, (rc, d_cm)).reshape(groups, SUBLANES, d_cm)
        for k in range(CONV_LONG):
            q, r = divmod(first_tap + k, SUBLANES)
            src = ext_s if r == 0 else sh_s.at[r - 1]
            tap = src[pl.ds(r0 + q * SUBLANES, rc), :].reshape(groups, SUBLANES, d_cm)
            wk = dww_ref[k * SUBLANES:(k + 1) * SUBLANES, :]
            acc = acc + wk[None] * tap
        d = acc.reshape(rc, d_cm)
        mu = jnp.sum(d, axis=-1, keepdims=True) * (1.0 / d_cm)
        dc = d - mu
        var = jnp.sum(dc * dc, axis=-1, keepdims=True) * (1.0 / d_cm)
        ln = dc * lax.rsqrt(var + EPS) * lng + lnb
        y = ln * _sigmoid(ln)
        y_s[pl.ds(r0, rc), d_lru:] = (y * _rms_scale(y, d_cm) * onc).astype(jnp.bfloat16)
        return carry
    step, n = _row_chunks(ts, conv_body)
    lax.fori_loop(0, n, step, 0)

    o_ref[0] = x_ref[0] + jnp.dot(y_s[...], w_out_ref[...], preferred_element_type=jnp.float32)

    proj_s[0:SUBLANES, 0:d_lru] = proj_s[ts:ts + SUBLANES, 0:d_lru]
    ext_s[0:HALO_LONG, :] = ext_s[ts:ts + HALO_LONG, :]


def _ffn_kernel(x_ref, nffn_ref, wg_ref, wu_ref, wd_ref, nfin_ref, o_ref, f_s, *, final_norm):
    tm, d_model = x_ref.shape
    d_ff = wg_ref.shape[1]
    nffn = nffn_ref[...]
    nfin = nfin_ref[...]
    for s in range(tm // FFN_SUB_ROWS):
        rows = slice(s * FFN_SUB_ROWS, (s + 1) * FFN_SUB_ROWS)
        xv = x_ref[rows, :]
        h = (xv * _rms_scale(xv, d_model) * nffn).astype(jnp.bfloat16)
        for c in range(d_ff // FFN_COL_CHUNK):
            cols = slice(c * FFN_COL_CHUNK, (c + 1) * FFN_COL_CHUNK)
            g = jnp.dot(h, wg_ref[:, cols], preferred_element_type=jnp.float32)
            u = jnp.dot(h, wu_ref[:, cols], preferred_element_type=jnp.float32)
            f_s[rows, cols] = (_silu(g) * u).astype(jnp.bfloat16)
        y = xv + jnp.dot(f_s[rows, :], wd_ref[...], preferred_element_type=jnp.float32)
        if final_norm:
            y = y * _rms_scale(y, d_model) * nfin
        o_ref[rows, :] = y


def _resident(shape):
    zeros = (0,) * len(shape)
    return pl.BlockSpec(shape, lambda *_: zeros, pipeline_mode=pl.Buffered(1))


def _block_diag(w):
    heads, d, _ = w.shape
    eye = jnp.eye(heads, dtype=w.dtype)
    return jnp.einsum('hij,hg->higj', w, eye).reshape(heads * d, heads * d)


def _mixer_layer(x, norm_mix, w_in, conv4_w, conv4_b, gate_a_w, gate_a_b, gate_x_w, gate_x_b,
                 lru_lambda, dw31_w, dw31_b, cm_ln_g, cm_ln_b, out_norm_lru, out_norm_cm, w_out):
    bsz, seq, d_model = x.shape
    d_lru = conv4_b.shape[0]
    d_cm = dw31_b.shape[0]
    ts = SEQ_TILE
    assert seq % ts == 0 and ts % ROW_CHUNK == 0
    assert conv4_w.shape[0] == CONV_SHORT and dw31_w.shape[0] == CONV_LONG
    bf16 = jnp.bfloat16
    row = lambda v: v.reshape(1, -1)
    w_gate = jnp.concatenate([_block_diag(gate_a_w), _block_diag(gate_x_w)], axis=1).astype(bf16)
    dww_rows = jnp.repeat(dw31_w, SUBLANES, axis=0)
    operands = (
        x, row(norm_mix), w_in.astype(bf16), conv4_w, row(conv4_b), w_gate, row(gate_a_b),
        row(gate_x_b), row(lru_lambda), dww_rows, row(dw31_b), row(cm_ln_g), row(cm_ln_b),
        row(out_norm_lru), row(out_norm_cm), w_out.astype(bf16))
    in_specs = [pl.BlockSpec((1, ts, d_model), lambda b, j: (b, j, 0))]
    in_specs += [_resident(op.shape) for op in operands[1:]]
    f32 = jnp.float32
    scratch = [
        pltpu.VMEM((ts, d_model), bf16),
        pltpu.VMEM((ts + SUBLANES, w_in.shape[1]), f32),
        pltpu.VMEM((ts, d_lru), f32),
        pltpu.VMEM((ts, d_lru), bf16),
        pltpu.VMEM((ts, 2 * d_lru), f32),
        pltpu.VMEM((ts + HALO_LONG, d_cm), f32),
        pltpu.VMEM((SUBLANES - 1, ts + HALO_LONG - SUBLANES, d_cm), f32),
        pltpu.VMEM((ts, d_lru + d_cm), bf16),
        pltpu.VMEM((SUBLANES, d_lru), f32),
    ]
    return pl.pallas_call(
        _mixer_kernel,
        grid=(bsz, seq // ts),
        in_specs=in_specs,
        out_specs=pl.BlockSpec((1, ts, d_model), lambda b, j: (b, j, 0)),
        out_shape=jax.ShapeDtypeStruct(x.shape, x.dtype),
        scratch_shapes=scratch,
        compiler_params=pltpu.CompilerParams(
            dimension_semantics=("arbitrary", "arbitrary"),
            vmem_limit_bytes=VMEM_LIMIT_BYTES),
        name="mixer",
    )(*operands)


def _ffn_layer(x, norm_ffn, w_gate, w_up, w_down, norm_final, final_norm):
    bsz, seq, d_model = x.shape
    d_ff = w_gate.shape[1]
    n_tok = bsz * seq
    tm = TOK_TILE
    assert n_tok % tm == 0 and tm % FFN_SUB_ROWS == 0 and d_ff % FFN_COL_CHUNK == 0
    bf16 = jnp.bfloat16
    operands = (x.reshape(n_tok, d_model), norm_ffn.reshape(1, -1), w_gate.astype(bf16),
                w_up.astype(bf16), w_down.astype(bf16), norm_final.reshape(1, -1))
    in_specs = [pl.BlockSpec((tm, d_model), lambda i: (i, 0))]
    in_specs += [_resident(op.shape) for op in operands[1:]]
    out = pl.pallas_call(
        functools.partial(_ffn_kernel, final_norm=final_norm),
        grid=(n_tok // tm,),
        in_specs=in_specs,
        out_specs=pl.BlockSpec((tm, d_model), lambda i: (i, 0)),
        out_shape=jax.ShapeDtypeStruct((n_tok, d_model), x.dtype),
        scratch_shapes=[pltpu.VMEM((tm, d_ff), bf16)],
        compiler_params=pltpu.CompilerParams(
            dimension_semantics=("arbitrary",),
            vmem_limit_bytes=VMEM_LIMIT_BYTES),
        name="ffn",
    )(*operands)
    return out.reshape(bsz, seq, d_model)


def kernel(x, norm_mix, w_in, conv4_w, conv4_b, gate_a_w, gate_a_b, gate_x_w, gate_x_b, lru_lambda, dw31_w, dw31_b, cm_ln_g, cm_ln_b, out_norm_lru, out_norm_cm, w_out, norm_ffn, w_gate, w_up, w_down, norm_final):
    depth = w_in.shape[0]
    for l in range(depth):
        x = _mixer_layer(x, norm_mix[l], w_in[l], conv4_w[l], conv4_b[l], gate_a_w[l], gate_a_b[l],
                         gate_x_w[l], gate_x_b[l], lru_lambda[l], dw31_w[l], dw31_b[l], cm_ln_g[l],
                         cm_ln_b[l], out_norm_lru[l], out_norm_cm[l], w_out[l])
        x = _ffn_layer(x, norm_ffn[l], w_gate[l], w_up[l], w_down[l], norm_final,
                       final_norm=(l == depth - 1))
    return x
```

```python
import functools
import math

import jax
import jax.numpy as jnp
from jax import lax
from jax.experimental import pallas as pl
from jax.experimental.pallas import tpu as pltpu

EPS = 1e-6
LRU_C = 8.0
SUBLANES = 8
CONV_SHORT = 4
CONV_LONG = 31
HALO_LONG = 32
SEQ_TILE = 512
TOK_TILE = 512
ROW_CHUNK = 32
FFN_SUB_ROWS = 256
FFN_COL_CHUNK = 256
VMEM_LIMIT_BYTES = 56 * 1024 * 1024


def _sigmoid(v):
    return 0.5 + 0.5 * jnp.tanh(0.5 * v)


def _silu(v):
    hv = 0.5 * v
    return hv + hv * jnp.tanh(hv)


def _gelu_tanh(v):
    inner = math.sqrt(2.0 / math.pi) * (v + 0.044715 * (v * v * v))
    return v * (0.5 * (1.0 + jnp.tanh(inner)))


def _rms_scale(v, width):
    return lax.rsqrt(jnp.sum(v * v, axis=-1, keepdims=True) * (1.0 / width) + EPS)


def _for_row_chunks(n_rows, body, carry=None):
    for r0 in range(0, n_rows, ROW_CHUNK):
        carry = body(r0, carry)
    return carry


def _mixer_kernel(x_ref, nmix_ref, w_in_ref, c4w_ref, c4b_ref, wgate_ref, gab_ref, gxb_ref,
                  lam_ref, dww_ref, dwb_ref, lng_ref, lnb_ref, onl_ref, onc_ref, w_out_ref,
                  o_ref,
                  h_s, proj_s, c_s, cb_s, gates_s, ext_s, sh_s, y_s, hc_s):
    ts = x_ref.shape[1]
    d_model = x_ref.shape[2]
    d_lru = c4b_ref.shape[1]
    d_cm = dwb_ref.shape[1]
    rc = ROW_CHUNK
    j = pl.program_id(1)

    @pl.when(j == 0)
    def _():
        proj_s[0:SUBLANES, 0:d_lru] = jnp.zeros((SUBLANES, d_lru), jnp.float32)
        ext_s[0:HALO_LONG, :] = jnp.zeros((HALO_LONG, d_cm), jnp.float32)
        hc_s[...] = jnp.zeros_like(hc_s)

    nmix = nmix_ref[...]

    def norm_body(r0, carry):
        xv = x_ref[0, pl.ds(r0, rc), :]
        h_s[pl.ds(r0, rc), :] = (xv * _rms_scale(xv, d_model) * nmix).astype(jnp.bfloat16)
        return carry
    _for_row_chunks(ts, norm_body)

    proj_s[pl.ds(SUBLANES, ts), :] = jnp.dot(h_s[...], w_in_ref[...],
                                             preferred_element_type=jnp.float32)

    c4w = c4w_ref[...]
    c4b = c4b_ref[...]

    def pre_body(r0, carry):
        win = proj_s[pl.ds(r0, rc + SUBLANES), 0:d_lru]
        acc = c4b + c4w[CONV_SHORT - 1:CONV_SHORT, :] * win[SUBLANES:, :]
        for s in range(1, CONV_SHORT):
            tap = pltpu.roll(win, s, axis=0)[SUBLANES:, :]
            acc = acc + c4w[CONV_SHORT - 1 - s:CONV_SHORT - s, :] * tap
        c_s[pl.ds(r0, rc), :] = acc
        cb_s[pl.ds(r0, rc), :] = acc.astype(jnp.bfloat16)
        v = proj_s[pl.ds(r0 + SUBLANES, rc), 2 * d_lru:2 * d_lru + d_cm]
        g = proj_s[pl.ds(r0 + SUBLANES, rc), 2 * d_lru + d_cm:]
        ext_s[pl.ds(r0 + HALO_LONG, rc), :] = v * _sigmoid(g)
        return carry
    _for_row_chunks(ts, pre_body)

    def shift_rows(r0, n_rows):
        win = ext_s[pl.ds(r0, n_rows + SUBLANES), :]
        for r in range(1, SUBLANES):
            rolled = pltpu.roll(win, n_rows + SUBLANES - r, axis=0)
            sh_s[r - 1, pl.ds(r0, n_rows), :] = rolled[0:n_rows, :]

    def shift_body(r0, carry):
        shift_rows(r0, rc)
        return carry
    _for_row_chunks(ts, shift_body)
    shift_rows(ts, HALO_LONG - SUBLANES)

    gates_s[...] = jnp.dot(cb_s[...], wgate_ref[...], preferred_element_type=jnp.float32)

    gab = gab_ref[...]
    gxb = gxb_ref[...]
    z = -lam_ref[...]
    neg_c_softplus = -LRU_C * (jnp.maximum(z, 0.0) + jnp.log1p(jnp.exp(-jnp.abs(z))))
    onl = onl_ref[...]
    groups = rc // SUBLANES
    row_in_group = lax.broadcasted_iota(jnp.int32, (groups, SUBLANES, d_lru), 1)

    def lru_body(r0, hprev):
        cv = c_s[pl.ds(r0, rc), :]
        r_gate = _sigmoid(gates_s[pl.ds(r0, rc), 0:d_lru] + gab)
        i_gate = _sigmoid(gates_s[pl.ds(r0, rc), d_lru:] + gxb)
        log_a = neg_c_softplus * r_gate
        a = jnp.exp(log_a)
        b = jnp.sqrt(-jnp.tanh(log_a) * (a * a + 1.0)) * (i_gate * cv)
        a3 = a.reshape(groups, SUBLANES, d_lru)
        b3 = b.reshape(groups, SUBLANES, d_lru)
        for s in (1, 2, 4):
            keep = row_in_group >= s
            a_prev = jnp.where(keep, pltpu.roll(a3, s, axis=1), 1.0)
            b_prev = jnp.where(keep, pltpu.roll(b3, s, axis=1), 0.0)
            b3 = a3 * b_prev + b3
            a3 = a3 * a_prev
        hs = []
        for gi in range(groups):
            hg = b3[gi] + a3[gi] * hprev
            hs.append(hg)
            hprev = jnp.broadcast_to(hg[SUBLANES - 1:SUBLANES, :], (SUBLANES, d_lru))
        hv = jnp.concatenate(hs, axis=0)
        y = hv * _gelu_tanh(proj_s[pl.ds(r0 + SUBLANES, rc), d_lru:2 * d_lru])
        y_s[pl.ds(r0, rc), 0:d_lru] = (y * _rms_scale(y, d_lru) * onl).astype(jnp.bfloat16)
        return hprev
    hc_s[...] = _for_row_chunks(ts, lru_body, hc_s[...])

    dwb = dwb_ref[...]
    lng = lng_ref[...]
    lnb = lnb_ref[...]
    onc = onc_ref[...]
    first_tap = HALO_LONG - (CONV_LONG - 1)

    def conv_body(r0, carry):
        acc = jnp.broadcast_to(dwb, (rc, d_cm)).reshape(groups, SUBLANES, d_cm)
        for k in range(CONV_LONG):
            q, r = divmod(first_tap + k, SUBLANES)
            src = ext_s if r == 0 else sh_s.at[r - 1]
            tap = src[pl.ds(r0 + q * SUBLANES, rc), :].reshape(groups, SUBLANES, d_cm)
            wk = dww_ref[k * SUBLANES:(k + 1) * SUBLANES, :]
            acc = acc + wk[None] * tap
        d = acc.reshape(rc, d_cm)
        mu = jnp.sum(d, axis=-1, keepdims=True) * (1.0 / d_cm)
        dc = d - mu
        var = jnp.sum(dc * dc, axis=-1, keepdims=True) * (1.0 / d_cm)
        ln = dc * lax.rsqrt(var + EPS) * lng + lnb
        y = ln * _sigmoid(ln)
        y_s[pl.ds(r0, rc), d_lru:] = (y * _rms_scale(y, d_cm) * onc).astype(jnp.bfloat16)
        return carry
    _for_row_chunks(ts, conv_body)

    o_ref[0] = x_ref[0] + jnp.dot(y_s[...], w_out_ref[...], preferred_element_type=jnp.float32)

    proj_s[0:SUBLANES, 0:d_lru] = proj_s[ts:ts + SUBLANES, 0:d_lru]
    ext_s[0:HALO_LONG, :] = ext_s[ts:ts + HALO_LONG, :]


def _ffn_kernel(x_ref, nffn_ref, wg_ref, wu_ref, wd_ref, nfin_ref, o_ref, f_s, *, final_norm):
    tm, d_model = x_ref.shape
    d_ff = wg_ref.shape[1]
    nffn = nffn_ref[...]
    nfin = nfin_ref[...]
    for s in range(tm // FFN_SUB_ROWS):
        rows = slice(s * FFN_SUB_ROWS, (s + 1) * FFN_SUB_ROWS)
        xv = x_ref[rows, :]
        h = (xv * _rms_scale(xv, d_model) * nffn).astype(jnp.bfloat16)
        for c in range(d_ff // FFN_COL_CHUNK):
            cols = slice(c * FFN_COL_CHUNK, (c + 1) * FFN_COL_CHUNK)
            g = jnp.dot(h, wg_ref[:, cols], preferred_element_type=jnp.float32)
            u = jnp.dot(h, wu_ref[:, cols], preferred_element_type=jnp.float32)
            f_s[rows, cols] = (_silu(g) * u).astype(jnp.bfloat16)
        y = xv + jnp.dot(f_s[rows, :], wd_ref[...], preferred_element_type=jnp.float32)
        if final_norm:
            y = y * _rms_scale(y, d_model) * nfin
        o_ref[rows, :] = y


def _resident(shape):
    zeros = (0,) * len(shape)
    return pl.BlockSpec(shape, lambda *_: zeros, pipeline_mode=pl.Buffered(1))


def _block_diag(w):
    heads, d, _ = w.shape
    eye = jnp.eye(heads, dtype=w.dtype)
    return jnp.einsum('hij,hg->higj', w, eye).reshape(heads * d, heads * d)


def _mixer_layer(x, norm_mix, w_in, conv4_w, conv4_b, gate_a_w, gate_a_b, gate_x_w, gate_x_b,
                 lru_lambda, dw31_w, dw31_b, cm_ln_g, cm_ln_b, out_norm_lru, out_norm_cm, w_out):
    bsz, seq, d_model = x.shape
    d_lru = conv4_b.shape[0]
    d_cm = dw31_b.shape[0]
    ts = SEQ_TILE
    assert seq % ts == 0 and ts % ROW_CHUNK == 0
    assert conv4_w.shape[0] == CONV_SHORT and dw31_w.shape[0] == CONV_LONG
    bf16 = jnp.bfloat16
    row = lambda v: v.reshape(1, -1)
    w_gate = jnp.concatenate([_block_diag(gate_a_w), _block_diag(gate_x_w)], axis=1).astype(bf16)
    dww_rows = jnp.repeat(dw31_w, SUBLANES, axis=0)
    operands = (
        x, row(norm_mix), w_in.astype(bf16), conv4_w, row(conv4_b), w_gate, row(gate_a_b),
        row(gate_x_b), row(lru_lambda), dww_rows, row(dw31_b), row(cm_ln_g), row(cm_ln_b),
        row(out_norm_lru), row(out_norm_cm), w_out.astype(bf16))
    in_specs = [pl.BlockSpec((1, ts, d_model), lambda b, j: (b, j, 0))]
    in_specs += [_resident(op.shape) for op in operands[1:]]
    f32 = jnp.float32
    scratch = [
        pltpu.VMEM((ts, d_model), bf16),
        pltpu.VMEM((ts + SUBLANES, w_in.shape[1]), f32),
        pltpu.VMEM((ts, d_lru), f32),
        pltpu.VMEM((ts, d_lru), bf16),
        pltpu.VMEM((ts, 2 * d_lru), f32),
        pltpu.VMEM((ts + HALO_LONG, d_cm), f32),
        pltpu.VMEM((SUBLANES - 1, ts + HALO_LONG - SUBLANES, d_cm), f32),
        pltpu.VMEM((ts, d_lru + d_cm), bf16),
        pltpu.VMEM((SUBLANES, d_lru), f32),
    ]
    return pl.pallas_call(
        _mixer_kernel,
        grid=(bsz, seq // ts),
        in_specs=in_specs,
        out_specs=pl.BlockSpec((1, ts, d_model), lambda b, j: (b, j, 0)),
        out_shape=jax.ShapeDtypeStruct(x.shape, x.dtype),
        scratch_shapes=scratch,
        compiler_params=pltpu.CompilerParams(
            dimension_semantics=("arbitrary", "arbitrary"),
            vmem_limit_bytes=VMEM_LIMIT_BYTES),
        name="mixer",
    )(*operands)


def _ffn_layer(x, norm_ffn, w_gate, w_up, w_down, norm_final, final_norm):
    bsz, seq, d_model = x.shape
    d_ff = w_gate.shape[1]
    n_tok = bsz * seq
    tm = TOK_TILE
    assert n_tok % tm == 0 and tm % FFN_SUB_ROWS == 0 and d_ff % FFN_COL_CHUNK == 0
    bf16 = jnp.bfloat16
    operands = (x.reshape(n_tok, d_model), norm_ffn.reshape(1, -1), w_gate.astype(bf16),
                w_up.astype(bf16), w_down.astype(bf16), norm_final.reshape(1, -1))
    in_specs = [pl.BlockSpec((tm, d_model), lambda i: (i, 0))]
    in_specs += [_resident(op.shape) for op in operands[1:]]
    out = pl.pallas_call(
        functools.partial(_ffn_kernel, final_norm=final_norm),
        grid=(n_tok // tm,),
        in_specs=in_specs,
        out_specs=pl.BlockSpec((tm, d_model), lambda i: (i, 0)),
        out_shape=jax.ShapeDtypeStruct((n_tok, d_model), x.dtype),
        scratch_shapes=[pltpu.VMEM((tm, d_ff), bf16)],
        compiler_params=pltpu.CompilerParams(
            dimension_semantics=("arbitrary",),
            vmem_limit_bytes=VMEM_LIMIT_BYTES),
        name="ffn",
    )(*operands)
    return out.reshape(bsz, seq, d_model)


def kernel(x, norm_mix, w_in, conv4_w, conv4_b, gate_a_w, gate_a_b, gate_x_w, gate_x_b, lru_lambda, dw31_w, dw31_b, cm_ln_g, cm_ln_b, out_norm_lru, out_norm_cm, w_out, norm_ffn, w_gate, w_up, w_down, norm_final):
    depth = w_in.shape[0]
    for l in range(depth):
        x = _mixer_layer(x, norm_mix[l], w_in[l], conv4_w[l], conv4_b[l], gate_a_w[l], gate_a_b[l],
                         gate_x_w[l], gate_x_b[l], lru_lambda[l], dw31_w[l], dw31_b[l], cm_ln_g[l],
                         cm_ln_b[l], out_norm_lru[l], out_norm_cm[l], w_out[l])
        x = _ffn_layer(x, norm_ffn[l], w_gate[l], w_up[l], w_down[l], norm_final,
                       final_norm=(l == depth - 1))
    return x
```

```python
import functools
import math

import jax
import jax.numpy as jnp
from jax import lax
from jax.experimental import pallas as pl
from jax.experimental.pallas import tpu as pltpu

EPS = 1e-6
LRU_C = 8.0
SUBLANES = 8
LANES = 128
CONV_SHORT = 4
CONV_LONG = 31
HALO_LONG = 32
TOK_TILE = 256
ROW_CHUNK = 32
FFN_COL_CHUNK = 256
VMEM_LIMIT_BYTES = 56 * 1024 * 1024


def _sigmoid(v):
    return 0.5 + 0.5 * jnp.tanh(0.5 * v)


def _silu(v):
    hv = 0.5 * v
    return hv + hv * jnp.tanh(hv)


def _gelu_tanh(v):
    inner = math.sqrt(2.0 / math.pi) * (v + 0.044715 * (v * v * v))
    return v * (0.5 * (1.0 + jnp.tanh(inner)))


def _rms_scale(v, width):
    return lax.rsqrt(jnp.sum(v * v, axis=-1, keepdims=True) * (1.0 / width) + EPS)


def _tie(dst, src):
    if src is None:
        return dst
    dst3 = dst.reshape(-1, SUBLANES, LANES)
    never = jnp.abs(src) < -1.0
    return jnp.where(never[None], src[None], dst3).reshape(dst.shape)


def _spread(n_items, n_bins):
    return [n_items // n_bins + (1 if i < n_items % n_bins else 0) for i in range(n_bins)]


def _mixer_pieces(x_ref, x1_ref, nmix_ref, w_in_ref, c4w_ref, c4b_ref, wgate_ref, gab_ref, gxb_ref,
                  lam_ref, dww_ref, dwb_ref, lng_ref, lnb_ref, onl_ref, onc_ref, w_out_ref,
                  h_s, proj_s, c_s, cb_s, gates_s, ext_s, sh_s, y_s):
    ts, d_model = x_ref.shape
    d_lru = c4b_ref.shape[1]
    d_cm = dwb_ref.shape[1]
    rc = ROW_CHUNK
    groups = rc // SUBLANES
    first_tap = HALO_LONG - (CONV_LONG - 1)
    col_xl = slice(0, d_lru)
    col_gl = slice(d_lru, 2 * d_lru)
    col_vg = slice(2 * d_lru, 2 * d_lru + 2 * d_cm)

    def norm():
        nmix = nmix_ref[...]
        for r0 in range(0, ts, rc):
            xv = x_ref[pl.ds(r0, rc), :]
            h_s[pl.ds(r0, rc), :] = (xv * _rms_scale(xv, d_model) * nmix).astype(jnp.bfloat16)

    def in_proj(cols):
        proj_s[pl.ds(SUBLANES, ts), cols] = jnp.dot(h_s[...], w_in_ref[:, cols],
                                                    preferred_element_type=jnp.float32)

    def slab_cols(width):
        return [slice(ls, ls + LANES) for ls in range(0, width, LANES)]

    def conv4_glu(r0, after=None):
        for cols in slab_cols(d_lru):
            c4w = c4w_ref[:, cols]
            win = proj_s[pl.ds(r0, rc + SUBLANES), cols]
            win = _tie(win, after)
            acc = c4b_ref[:, cols] + c4w[CONV_SHORT - 1:CONV_SHORT, :] * win[SUBLANES:, :]
            for s in range(1, CONV_SHORT):
                tap = pltpu.roll(win, s, axis=0)[SUBLANES:, :]
                acc = acc + c4w[CONV_SHORT - 1 - s:CONV_SHORT - s, :] * tap
            c_s[pl.ds(r0, rc), cols] = acc
            cb_s[pl.ds(r0, rc), cols] = acc.astype(jnp.bfloat16)
        for cols in slab_cols(d_cm):
            v = proj_s[pl.ds(r0 + SUBLANES, rc), pl.ds(2 * d_lru + cols.start, LANES)]
            g = proj_s[pl.ds(r0 + SUBLANES, rc), pl.ds(2 * d_lru + d_cm + cols.start, LANES)]
            ext_s[pl.ds(r0 + HALO_LONG, rc), cols] = v * _sigmoid(g)

    def shift(r0, n_rows=rc):
        for cols in slab_cols(d_cm):
            win = ext_s[pl.ds(r0, n_rows + SUBLANES), cols]
            for r in range(1, SUBLANES):
                rolled = pltpu.roll(win, n_rows + SUBLANES - r, axis=0)
                sh_s[r - 1, pl.ds(r0, n_rows), cols] = rolled[0:n_rows, :]

    def gates():
        gates_s[...] = jnp.dot(cb_s[...], wgate_ref[...], preferred_element_type=jnp.float32)

    def lru(r0, hprev, after=None):
        hprev = [_tie(hp, after) for hp in hprev]
        row_in_group = lax.broadcasted_iota(jnp.int32, (groups, SUBLANES, LANES), 1)
        ys, hnext = [], []
        for cols, hp in zip(slab_cols(d_lru), hprev):
            z = -lam_ref[:, cols]
            neg_c_softplus = -LRU_C * (jnp.maximum(z, 0.0) + jnp.log1p(jnp.exp(-jnp.abs(z))))
            cv = c_s[pl.ds(r0, rc), cols]
            r_gate = _sigmoid(gates_s[pl.ds(r0, rc), cols] + gab_ref[:, cols])
            i_gate = _sigmoid(gates_s[pl.ds(r0, rc), pl.ds(d_lru + cols.start, LANES)]
                              + gxb_ref[:, cols])
            log_a = neg_c_softplus * r_gate
            a = jnp.exp(log_a)
            b = jnp.sqrt(-jnp.tanh(log_a) * (a * a + 1.0)) * (i_gate * cv)
            a3 = a.reshape(groups, SUBLANES, LANES)
            b3 = b.reshape(groups, SUBLANES, LANES)
            for s in (1, 2, 4):
                keep = row_in_group >= s
                a_prev = jnp.where(keep, pltpu.roll(a3, s, axis=1), 1.0)
                b_prev = jnp.where(keep, pltpu.roll(b3, s, axis=1), 0.0)
                b3 = a3 * b_prev + b3
                a3 = a3 * a_prev
            hs = []
            for gi in range(groups):
                hg = b3[gi] + a3[gi] * hp
                hs.append(hg)
                hp = jnp.broadcast_to(hg[SUBLANES - 1:SUBLANES, :], (SUBLANES, LANES))
            hnext.append(hp)
            gl = proj_s[pl.ds(r0 + SUBLANES, rc), pl.ds(d_lru + cols.start, LANES)]
            ys.append(jnp.concatenate(hs, axis=0) * _gelu_tanh(gl))
        scale = lax.rsqrt(sum(jnp.sum(y * y, axis=-1, keepdims=True) for y in ys)
                          * (1.0 / d_lru) + EPS)
        for cols, y in zip(slab_cols(d_lru), ys):
            y_s[pl.ds(r0, rc), cols] = (y * scale * onl_ref[:, cols]).astype(jnp.bfloat16)
        return hnext

    def conv31(r0, after=None):
        ds = []
        for cols in slab_cols(d_cm):
            acc = jnp.broadcast_to(dwb_ref[:, cols], (rc, LANES))
            acc = _tie(acc, after).reshape(groups, SUBLANES, LANES)
            for k in range(CONV_LONG):
                q, r = divmod(first_tap + k, SUBLANES)
                src = ext_s if r == 0 else sh_s.at[r - 1]
                tap = src[pl.ds(r0 + q * SUBLANES, rc), cols].reshape(groups, SUBLANES, LANES)
                wk = dww_ref[k * SUBLANES:(k + 1) * SUBLANES, cols]
                acc = acc + wk[None] * tap
            ds.append(acc.reshape(rc, LANES))
        mu = sum(jnp.sum(d, axis=-1, keepdims=True) for d in ds) * (1.0 / d_cm)
        dcs = [d - mu for d in ds]
        var = sum(jnp.sum(dc * dc, axis=-1, keepdims=True) for dc in dcs) * (1.0 / d_cm)
        inv = lax.rsqrt(var + EPS)
        ys = []
        for cols, dc in zip(slab_cols(d_cm), dcs):
            ln = dc * inv * lng_ref[:, cols] + lnb_ref[:, cols]
            ys.append(ln * _sigmoid(ln))
        scale = lax.rsqrt(sum(jnp.sum(y * y, axis=-1, keepdims=True) for y in ys)
                          * (1.0 / d_cm) + EPS)
        for cols, y in zip(slab_cols(d_cm), ys):
            y_s[pl.ds(r0, rc), pl.ds(d_lru + cols.start, LANES)] = (
                y * scale * onc_ref[:, cols]).astype(jnp.bfloat16)

    def out_proj():
        x1_ref[...] = x_ref[...] + jnp.dot(y_s[...], w_out_ref[...],
                                           preferred_element_type=jnp.float32)
        proj_s[0:SUBLANES, col_xl] = proj_s[ts:ts + SUBLANES, col_xl]
        ext_s[0:HALO_LONG, :] = ext_s[ts:ts + HALO_LONG, :]

    return dict(norm=norm, in_proj=in_proj, conv4_glu=conv4_glu, shift=shift, gates=gates,
                lru=lru, conv31=conv31, out_proj=out_proj,
                col_xl=col_xl, col_gl=col_gl, col_vg=col_vg)


def _ffn_pieces(x_ref, o_ref, nffn_ref, wg_ref, wu_ref, wd_ref, nfin_ref, hf_s, f_s, yf_s,
                *, final_norm):
    ts, d_model = x_ref.shape

    def norm():
        xv = x_ref[...]
        hf_s[...] = (xv * _rms_scale(xv, d_model) * nffn_ref[...]).astype(jnp.bfloat16)

    def gate_up(c):
        cols = slice(c * FFN_COL_CHUNK, (c + 1) * FFN_COL_CHUNK)
        h = hf_s[...]
        g = jnp.dot(h, wg_ref[:, cols], preferred_element_type=jnp.float32)
        u = jnp.dot(h, wu_ref[:, cols], preferred_element_type=jnp.float32)
        f = _silu(g) * u
        f_s[:, cols] = f.astype(jnp.bfloat16)
        return f[-SUBLANES:, -LANES:]

    def down(n):
        cols = slice(n * FFN_COL_CHUNK, (n + 1) * FFN_COL_CHUNK)
        y = x_ref[:, cols] + jnp.dot(f_s[...], wd_ref[:, cols],
                                     preferred_element_type=jnp.float32)
        yf_s[:, cols] = y
        return y[-SUBLANES:, -LANES:]

    def finish():
        y = yf_s[...]
        if final_norm:
            y = y * _rms_scale(y, d_model) * nfin_ref[...]
        o_ref[...] = y

    return dict(norm=norm, gate_up=gate_up, down=down, finish=finish)


def _block_kernel(x_ref, nmix_ref, w_in_ref, c4w_ref, c4b_ref, wgate_ref, gab_ref, gxb_ref,
                  lam_ref, dww_ref, dwb_ref, lng_ref, lnb_ref, onl_ref, onc_ref, w_out_ref,
                  nffn_ref, wg_ref, wu_ref, wd_ref, nfin_ref,
                  o_ref,
                  x1_s, hf_s, f_s, yf_s, h_s, proj_s, c_s, cb_s, gates_s, ext_s, sh_s, y_s, hc_s,
                  *, tiles_per_seq, final_norm):
    t = pl.program_id(0)
    ts, d_model = x_ref.shape
    d_lru = c4b_ref.shape[1]
    d_ff = wg_ref.shape[1]
    rc = ROW_CHUNK
    slot = lax.rem(t, 2)

    @pl.when(t == 0)
    def _():
        x1_s[1] = jnp.zeros(x1_s.shape[1:], jnp.float32)

    @pl.when(lax.rem(t, tiles_per_seq) == 0)
    def _():
        proj_s[0:SUBLANES, 0:d_lru] = jnp.zeros((SUBLANES, d_lru), jnp.float32)
        ext_s[0:HALO_LONG, :] = jnp.zeros((HALO_LONG, ext_s.shape[1]), jnp.float32)
        hc_s[...] = jnp.zeros_like(hc_s)

    mix = _mixer_pieces(x_ref, x1_s.at[slot], nmix_ref, w_in_ref, c4w_ref, c4b_ref, wgate_ref,
                        gab_ref, gxb_ref, lam_ref, dww_ref, dwb_ref, lng_ref, lnb_ref, onl_ref,
                        onc_ref, w_out_ref, h_s, proj_s, c_s, cb_s, gates_s, ext_s, sh_s, y_s)
    ffn = _ffn_pieces(x1_s.at[1 - slot], o_ref, nffn_ref, wg_ref, wu_ref, wd_ref, nfin_ref,
                      hf_s, f_s, yf_s, final_norm=final_norm)

    row_chunks = list(range(0, ts, rc))
    n_gate_up = d_ff // FFN_COL_CHUNK
    n_down = d_model // FFN_COL_CHUNK
    ffn_matmuls = iter([functools.partial(ffn["gate_up"], c) for c in range(n_gate_up)]
                       + [functools.partial(ffn["down"], n) for n in range(n_down)])

    def ffn_matmul(wanted=True):
        piece = next(ffn_matmuls, None) if wanted else None
        return None if piece is None else piece()

    ffn["norm"]()
    mix["norm"]()
    mix["in_proj"](mix["col_xl"])
    done = ffn_matmul()
    mix["in_proj"](mix["col_vg"])
    ffn_matmul()
    mix["in_proj"](mix["col_gl"])
    for i, r0 in enumerate(row_chunks):
        mix["conv4_glu"](r0, after=done)
        done = ffn_matmul(i in (2, 5))
        if i > 0:
            mix["shift"](row_chunks[i - 1])
    mix["shift"](row_chunks[-1])
    mix["shift"](ts, HALO_LONG - SUBLANES)
    mix["gates"]()
    hprev = [hc_s[:, ls:ls + LANES] for ls in range(0, d_lru, LANES)]
    for i, r0 in enumerate(row_chunks):
        mix["conv31"](r0, after=ffn_matmul())
        hprev = mix["lru"](r0, hprev, after=ffn_matmul(i % 2 == 0))
    hc_s[...] = jnp.concatenate(hprev, axis=1)
    while ffn_matmul() is not None:
        pass
    mix["out_proj"]()
    ffn["finish"]()


def _resident(shape):
    zeros = (0,) * len(shape)
    return pl.BlockSpec(shape, lambda *_: zeros, pipeline_mode=pl.Buffered(1))


def _block_diag(w):
    heads, d, _ = w.shape
    eye = jnp.eye(heads, dtype=w.dtype)
    return jnp.einsum('hij,hg->higj', w, eye).reshape(heads * d, heads * d)


def _layer(x, norm_mix, w_in, conv4_w, conv4_b, gate_a_w, gate_a_b, gate_x_w, gate_x_b,
           lru_lambda, dw31_w, dw31_b, cm_ln_g, cm_ln_b, out_norm_lru, out_norm_cm, w_out,
           norm_ffn, w_gate, w_up, w_down, norm_final, final_norm):
    bsz, seq, d_model = x.shape
    d_lru = conv4_b.shape[0]
    d_cm = dw31_b.shape[0]
    d_ff = w_gate.shape[1]
    ts = TOK_TILE
    n_tok = bsz * seq
    n_tiles = n_tok // ts
    assert seq % ts == 0 and ts % ROW_CHUNK == 0
    assert d_ff % FFN_COL_CHUNK == 0 and d_model % FFN_COL_CHUNK == 0
    assert conv4_w.shape[0] == CONV_SHORT and dw31_w.shape[0] == CONV_LONG
    bf16 = jnp.bfloat16
    f32 = jnp.float32
    row = lambda v: v.reshape(1, -1)
    w_gates = jnp.concatenate([_block_diag(gate_a_w), _block_diag(gate_x_w)], axis=1).astype(bf16)
    dww_rows = jnp.repeat(dw31_w, SUBLANES, axis=0)
    operands = (
        x.reshape(n_tok, d_model), row(norm_mix), w_in.astype(bf16), conv4_w, row(conv4_b),
        w_gates, row(gate_a_b), row(gate_x_b), row(lru_lambda), dww_rows, row(dw31_b),
        row(cm_ln_g), row(cm_ln_b), row(out_norm_lru), row(out_norm_cm), w_out.astype(bf16),
        row(norm_ffn), w_gate.astype(bf16), w_up.astype(bf16), w_down.astype(bf16),
        row(norm_final))
    in_specs = [pl.BlockSpec((ts, d_model), lambda t: (jnp.minimum(t, n_tiles - 1), 0))]
    in_specs += [_resident(op.shape) for op in operands[1:]]
    scratch = [
        pltpu.VMEM((2, ts, d_model), f32),
        pltpu.VMEM((ts, d_model), bf16),
        pltpu.VMEM((ts, d_ff), bf16),
        pltpu.VMEM((ts, d_model), f32),
        pltpu.VMEM((ts, d_model), bf16),
        pltpu.VMEM((ts + SUBLANES, w_in.shape[1]), f32),
        pltpu.VMEM((ts, d_lru), f32),
        pltpu.VMEM((ts, d_lru), bf16),
        pltpu.VMEM((ts, 2 * d_lru), f32),
        pltpu.VMEM((ts + HALO_LONG, d_cm), f32),
        pltpu.VMEM((SUBLANES - 1, ts + HALO_LONG - SUBLANES, d_cm), f32),
        pltpu.VMEM((ts, d_lru + d_cm), bf16),
        pltpu.VMEM((SUBLANES, d_lru), f32),
    ]
    out = pl.pallas_call(
        functools.partial(_block_kernel, tiles_per_seq=seq // ts, final_norm=final_norm),
        grid=(n_tiles + 1,),
        in_specs=in_specs,
        out_specs=pl.BlockSpec((ts, d_model), lambda t: (jnp.maximum(t - 1, 0), 0)),
        out_shape=jax.ShapeDtypeStruct((n_tok, d_model), x.dtype),
        scratch_shapes=scratch,
        compiler_params=pltpu.CompilerParams(
            dimension_semantics=("arbitrary",),
            vmem_limit_bytes=VMEM_LIMIT_BYTES),
        name="block",
    )(*operands)
    return out.reshape(bsz, seq, d_model)


def kernel(x, norm_mix, w_in, conv4_w, conv4_b, gate_a_w, gate_a_b, gate_x_w, gate_x_b, lru_lambda, dw31_w, dw31_b, cm_ln_g, cm_ln_b, out_norm_lru, out_norm_cm, w_out, norm_ffn, w_gate, w_up, w_down, norm_final):
    depth = w_in.shape[0]
    for l in range(depth):
        x = _layer(x, norm_mix[l], w_in[l], conv4_w[l], conv4_b[l], gate_a_w[l], gate_a_b[l],
                   gate_x_w[l], gate_x_b[l], lru_lambda[l], dw31_w[l], dw31_b[l], cm_ln_g[l],
                   cm_ln_b[l], out_norm_lru[l], out_norm_cm[l], w_out[l], norm_ffn[l], w_gate[l],
                   w_up[l], w_down[l], norm_final, final_norm=(l == depth - 1))
    return x
```

```python
import functools
import math

import jax
import jax.numpy as jnp
from jax import lax
from jax.experimental import pallas as pl
from jax.experimental.pallas import tpu as pltpu

EPS = 1e-6
LRU_C = 8.0
SUBLANES = 8
LANES = 128
MXU_DIM = 256
CONV_SHORT = 4
CONV_LONG = 31
HALO_LONG = 32
TOK_TILE = 256
ROW_CHUNK = 32
FFN_COL_CHUNK = 256
EARLY_FFN_PIECES = 4
VMEM_LIMIT_BYTES = 56 * 1024 * 1024


def _sigmoid(v):
    return 0.5 + 0.5 * jnp.tanh(0.5 * v)


def _silu(v):
    hv = 0.5 * v
    return hv + hv * jnp.tanh(hv)


def _gelu_tanh(v):
    inner = math.sqrt(2.0 / math.pi) * (v + 0.044715 * (v * v * v))
    return v * (0.5 * (1.0 + jnp.tanh(inner)))


def _rms_scale(v, width):
    return lax.rsqrt(jnp.sum(v * v, axis=-1, keepdims=True) * (1.0 / width) + EPS)


def _tie(dst, src):
    if src is None:
        return dst
    dst3 = dst.reshape(-1, SUBLANES, LANES)
    never = jnp.abs(src) < -1.0
    return jnp.where(never[None], src[None], dst3).reshape(dst.shape)


def _spread(n_items, n_bins):
    return [n_items // n_bins + (1 if i < n_items % n_bins else 0) for i in range(n_bins)]


def _mixer_pieces(x_ref, x1_ref, nmix_ref, w_in_ref, c4w_ref, c4b_ref, wgate_ref, gab_ref, gxb_ref,
                  lam_ref, dww_ref, dwb_ref, lng_ref, lnb_ref, onl_ref, onc_ref, w_out_ref,
                  h_s, proj_s, c_s, cb_s, gates_s, ext_s, sh_s, y_s):
    ts, d_model = x_ref.shape
    d_lru = c4b_ref.shape[1]
    d_cm = dwb_ref.shape[1]
    rc = ROW_CHUNK
    groups = rc // SUBLANES
    first_tap = HALO_LONG - (CONV_LONG - 1)
    col_xl = slice(0, d_lru)
    col_gl = slice(d_lru, 2 * d_lru)
    col_vg = slice(2 * d_lru, 2 * d_lru + 2 * d_cm)

    def norm():
        nmix = nmix_ref[...]
        for r0 in range(0, ts, rc):
            xv = x_ref[pl.ds(r0, rc), :]
            h_s[pl.ds(r0, rc), :] = (xv * _rms_scale(xv, d_model) * nmix).astype(jnp.bfloat16)

    def in_proj(cols):
        proj_s[pl.ds(SUBLANES, ts), cols] = jnp.dot(h_s[...], w_in_ref[:, cols],
                                                    preferred_element_type=jnp.float32)

    def slab_cols(width):
        return [slice(ls, ls + LANES) for ls in range(0, width, LANES)]

    def conv4_glu(r0, after=None):
        for cols in slab_cols(d_lru):
            c4w = c4w_ref[:, cols]
            win = proj_s[pl.ds(r0, rc + SUBLANES), cols]
            win = _tie(win, after)
            acc = c4b_ref[:, cols] + c4w[CONV_SHORT - 1:CONV_SHORT, :] * win[SUBLANES:, :]
            for s in range(1, CONV_SHORT):
                tap = pltpu.roll(win, s, axis=0)[SUBLANES:, :]
                acc = acc + c4w[CONV_SHORT - 1 - s:CONV_SHORT - s, :] * tap
            c_s[pl.ds(r0, rc), cols] = acc
            cb_s[pl.ds(r0, rc), cols] = acc.astype(jnp.bfloat16)
        for cols in slab_cols(d_cm):
            v = proj_s[pl.ds(r0 + SUBLANES, rc), pl.ds(2 * d_lru + cols.start, LANES)]
            g = proj_s[pl.ds(r0 + SUBLANES, rc), pl.ds(2 * d_lru + d_cm + cols.start, LANES)]
            ext_s[pl.ds(r0 + HALO_LONG, rc), cols] = v * _sigmoid(g)

    def shift(r0, n_rows=rc):
        for cols in slab_cols(d_cm):
            win = ext_s[pl.ds(r0, n_rows + SUBLANES), cols]
            for r in range(1, SUBLANES):
                rolled = pltpu.roll(win, n_rows + SUBLANES - r, axis=0)
                sh_s[r - 1, pl.ds(r0, n_rows), cols] = rolled[0:n_rows, :]

    def gates():
        blk = wgate_ref.shape[1]
        for b in range(wgate_ref.shape[0]):
            res = jnp.dot(cb_s[:, b * blk:(b + 1) * blk], wgate_ref[b],
                          preferred_element_type=jnp.float32)
            gates_s[:, b * blk:(b + 1) * blk] = res[:, :blk]
            gates_s[:, d_lru + b * blk:d_lru + (b + 1) * blk] = res[:, blk:]

    def lru(r0, hprev, after=None):
        row_in_group = lax.broadcasted_iota(jnp.int32, (groups, SUBLANES, LANES), 1)
        hprev = [_tie(hp, after) for hp in hprev]
        ys, hnext = [], []
        for cols, hp in zip(slab_cols(d_lru), hprev):
            z = -lam_ref[:, cols]
            neg_c_softplus = -LRU_C * (jnp.maximum(z, 0.0) + jnp.log1p(jnp.exp(-jnp.abs(z))))
            cv = c_s[pl.ds(r0, rc), cols]
            r_gate = _sigmoid(gates_s[pl.ds(r0, rc), cols] + gab_ref[:, cols])
            i_gate = _sigmoid(gates_s[pl.ds(r0, rc), pl.ds(d_lru + cols.start, LANES)]
                              + gxb_ref[:, cols])
            log_a = neg_c_softplus * r_gate
            a = jnp.exp(log_a)
            b = jnp.sqrt(-jnp.tanh(log_a) * (a * a + 1.0)) * (i_gate * cv)
            a3 = a.reshape(groups, SUBLANES, LANES)
            b3 = b.reshape(groups, SUBLANES, LANES)
            for s in (1, 2, 4):
                keep = row_in_group >= s
                a_prev = jnp.where(keep, pltpu.roll(a3, s, axis=1), 1.0)
                b_prev = jnp.where(keep, pltpu.roll(b3, s, axis=1), 0.0)
                b3 = a3 * b_prev + b3
                a3 = a3 * a_prev
            hs = []
            for gi in range(groups):
                hg = b3[gi] + a3[gi] * hp
                hs.append(hg)
                hp = jnp.broadcast_to(hg[SUBLANES - 1:SUBLANES, :], (SUBLANES, LANES))
            hnext.append(hp)
            gl = proj_s[pl.ds(r0 + SUBLANES, rc), pl.ds(d_lru + cols.start, LANES)]
            ys.append(jnp.concatenate(hs, axis=0) * _gelu_tanh(gl))
        scale = lax.rsqrt(sum(jnp.sum(y * y, axis=-1, keepdims=True) for y in ys)
                          * (1.0 / d_lru) + EPS)
        for cols, y in zip(slab_cols(d_lru), ys):
            y_s[pl.ds(r0, rc), cols] = (y * scale * onl_ref[:, cols]).astype(jnp.bfloat16)
        return hnext

    def conv31(r0, after=None):
        ds = []
        for cols in slab_cols(d_cm):
            acc = jnp.broadcast_to(dwb_ref[:, cols], (rc, LANES)).reshape(groups, SUBLANES, LANES)
            for k in range(CONV_LONG):
                q, r = divmod(first_tap + k, SUBLANES)
                src = ext_s if r == 0 else sh_s.at[r - 1]
                tap = src[pl.ds(r0 + q * SUBLANES, rc), cols].reshape(groups, SUBLANES, LANES)
                wk = _tie(dww_ref[k * SUBLANES:(k + 1) * SUBLANES, cols], after)
                acc = acc + wk[None] * tap
            ds.append(acc.reshape(rc, LANES))
        mu = sum(jnp.sum(d, axis=-1, keepdims=True) for d in ds) * (1.0 / d_cm)
        dcs = [d - mu for d in ds]
        var = sum(jnp.sum(dc * dc, axis=-1, keepdims=True) for dc in dcs) * (1.0 / d_cm)
        inv = lax.rsqrt(var + EPS)
        ys = []
        for cols, dc in zip(slab_cols(d_cm), dcs):
            ln = dc * inv * lng_ref[:, cols] + lnb_ref[:, cols]
            ys.append(ln * _sigmoid(ln))
        scale = lax.rsqrt(sum(jnp.sum(y * y, axis=-1, keepdims=True) for y in ys)
                          * (1.0 / d_cm) + EPS)
        for cols, y in zip(slab_cols(d_cm), ys):
            y_s[pl.ds(r0, rc), pl.ds(d_lru + cols.start, LANES)] = (
                y * scale * onc_ref[:, cols]).astype(jnp.bfloat16)

    def out_proj():
        x1_ref[...] = x_ref[...] + jnp.dot(y_s[...], w_out_ref[...],
                                           preferred_element_type=jnp.float32)
        proj_s[0:SUBLANES, col_xl] = proj_s[ts:ts + SUBLANES, col_xl]
        ext_s[0:HALO_LONG, :] = ext_s[ts:ts + HALO_LONG, :]

    return dict(norm=norm, in_proj=in_proj, conv4_glu=conv4_glu, shift=shift, gates=gates,
                lru=lru, conv31=conv31, out_proj=out_proj,
                col_xl=col_xl, col_gl=col_gl, col_vg=col_vg)


def _ffn_pieces(x_ref, o_ref, nffn_ref, wgu_ref, wd_ref, nfin_ref, hf_s, f_s, yf_s,
                *, final_norm):
    ts, d_model = x_ref.shape

    def norm():
        xv = x_ref[...]
        hf_s[...] = (xv * _rms_scale(xv, d_model) * nffn_ref[...]).astype(jnp.bfloat16)

    def gate_up(c):
        gu = jnp.dot(hf_s[...], wgu_ref[:, 2 * c * FFN_COL_CHUNK:2 * (c + 1) * FFN_COL_CHUNK],
                     preferred_element_type=jnp.float32)
        for j in range(FFN_COL_CHUNK // LANES):
            f = _silu(gu[:, 2 * j * LANES:(2 * j + 1) * LANES]) * gu[:, (2 * j + 1) * LANES:
                                                                      (2 * j + 2) * LANES]
            f_s[:, pl.ds(c * FFN_COL_CHUNK + j * LANES, LANES)] = f.astype(jnp.bfloat16)
        return f[-SUBLANES:, :]

    def down(n):
        cols = slice(n * FFN_COL_CHUNK, (n + 1) * FFN_COL_CHUNK)
        y = x_ref[:, cols] + jnp.dot(f_s[...], wd_ref[:, cols],
                                     preferred_element_type=jnp.float32)
        yf_s[:, cols] = y
        return y[-SUBLANES:, -LANES:]

    def finish():
        y = yf_s[...]
        if final_norm:
            y = y * _rms_scale(y, d_model) * nfin_ref[...]
        o_ref[...] = y

    return dict(norm=norm, gate_up=gate_up, down=down, finish=finish)


def _block_kernel(x_ref, nmix_ref, w_in_ref, c4w_ref, c4b_ref, wgate_ref, gab_ref, gxb_ref,
                  lam_ref, dww_ref, dwb_ref, lng_ref, lnb_ref, onl_ref, onc_ref, w_out_ref,
                  nffn_ref, wgu_ref, wd_ref, nfin_ref,
                  o_ref,
                  x1_s, hf_s, f_s, yf_s, h_s, proj_s, c_s, cb_s, gates_s, ext_s, sh_s, y_s, hc_s,
                  *, tiles_per_seq, final_norm):
    t = pl.program_id(0)
    ts, d_model = x_ref.shape
    d_lru = c4b_ref.shape[1]
    d_ff = wd_ref.shape[0]
    rc = ROW_CHUNK
    slot = lax.rem(t, 2)

    @pl.when(t == 0)
    def _():
        x1_s[1] = jnp.zeros(x1_s.shape[1:], jnp.float32)

    @pl.when(lax.rem(t, tiles_per_seq) == 0)
    def _():
        proj_s[0:SUBLANES, 0:d_lru] = jnp.zeros((SUBLANES, d_lru), jnp.float32)
        ext_s[0:HALO_LONG, :] = jnp.zeros((HALO_LONG, ext_s.shape[1]), jnp.float32)
        hc_s[...] = jnp.zeros_like(hc_s)

    mix = _mixer_pieces(x_ref, x1_s.at[slot], nmix_ref, w_in_ref, c4w_ref, c4b_ref, wgate_ref,
                        gab_ref, gxb_ref, lam_ref, dww_ref, dwb_ref, lng_ref, lnb_ref, onl_ref,
                        onc_ref, w_out_ref, h_s, proj_s, c_s, cb_s, gates_s, ext_s, sh_s, y_s)
    ffn = _ffn_pieces(x1_s.at[1 - slot], o_ref, nffn_ref, wgu_ref, wd_ref, nfin_ref,
                      hf_s, f_s, yf_s, final_norm=final_norm)

    row_chunks = list(range(0, ts, rc))
    n_gate_up = d_ff // FFN_COL_CHUNK
    n_down = d_model // FFN_COL_CHUNK
    ffn_matmuls = iter([functools.partial(ffn["gate_up"], c) for c in range(n_gate_up)]
                       + [functools.partial(ffn["down"], n) for n in range(n_down)])

    def ffn_matmul(wanted=True):
        piece = next(ffn_matmuls, None) if wanted else None
        return None if piece is None else piece()

    ffn["norm"]()
    mix["norm"]()
    mix["in_proj"](mix["col_xl"])
    ffn_matmul()
    mix["in_proj"](mix["col_vg"])
    ffn_matmul()
    mix["in_proj"](mix["col_gl"])
    for i, r0 in enumerate(row_chunks):
        mix["conv4_glu"](r0)
        if i > 0:
            mix["shift"](row_chunks[i - 1])
    mix["shift"](row_chunks[-1])
    mix["shift"](ts, HALO_LONG - SUBLANES)
    for _ in range(EARLY_FFN_PIECES - 2):
        ffn_matmul()
    mix["gates"]()
    hprev = [hc_s[:, ls:ls + LANES] for ls in range(0, d_lru, LANES)]
    for i, r0 in enumerate(row_chunks):
        mix["conv31"](r0, after=ffn_matmul())
        hprev = mix["lru"](r0, hprev, after=ffn_matmul(i % 2 == 0))
    hc_s[...] = jnp.concatenate(hprev, axis=1)
    while ffn_matmul() is not None:
        pass
    mix["out_proj"]()
    ffn["finish"]()


def _resident(shape):
    zeros = (0,) * len(shape)
    return pl.BlockSpec(shape, lambda *_: zeros, pipeline_mode=pl.Buffered(1))


def _block_diag(w, block):
    heads, d, _ = w.shape
    per_block = block // d
    eye = jnp.eye(per_block, dtype=w.dtype)
    w = w.reshape(heads // per_block, per_block, d, d)
    return jnp.einsum('nhij,hg->nhigj', w, eye).reshape(heads // per_block, block, block)


def _layer(x, norm_mix, w_in, conv4_w, conv4_b, gate_a_w, gate_a_b, gate_x_w, gate_x_b,
           lru_lambda, dw31_w, dw31_b, cm_ln_g, cm_ln_b, out_norm_lru, out_norm_cm, w_out,
           norm_ffn, w_gate, w_up, w_down, norm_final, final_norm):
    bsz, seq, d_model = x.shape
    d_lru = conv4_b.shape[0]
    d_cm = dw31_b.shape[0]
    d_ff = w_gate.shape[1]
    ts = TOK_TILE
    n_tok = bsz * seq
    n_tiles = n_tok // ts
    assert seq % ts == 0 and ts % ROW_CHUNK == 0
    assert d_ff % FFN_COL_CHUNK == 0 and d_model % FFN_COL_CHUNK == 0
    assert conv4_w.shape[0] == CONV_SHORT and dw31_w.shape[0] == CONV_LONG
    bf16 = jnp.bfloat16
    f32 = jnp.float32
    row = lambda v: v.reshape(1, -1)
    assert MXU_DIM % gate_a_w.shape[1] == 0 and d_lru % MXU_DIM == 0
    w_gates = jnp.concatenate([_block_diag(gate_a_w, MXU_DIM), _block_diag(gate_x_w, MXU_DIM)],
                              axis=2).astype(bf16)
    dww_rows = jnp.repeat(dw31_w, SUBLANES, axis=0)
    w_gate_up = jnp.stack([w_gate.reshape(d_model, d_ff // LANES, LANES),
                           w_up.reshape(d_model, d_ff // LANES, LANES)],
                          axis=2).reshape(d_model, 2 * d_ff).astype(bf16)
    operands = (
        x.reshape(n_tok, d_model), row(norm_mix), w_in.astype(bf16), conv4_w, row(conv4_b),
        w_gates, row(gate_a_b), row(gate_x_b), row(lru_lambda), dww_rows, row(dw31_b),
        row(cm_ln_g), row(cm_ln_b), row(out_norm_lru), row(out_norm_cm), w_out.astype(bf16),
        row(norm_ffn), w_gate_up, w_down.astype(bf16),
        row(norm_final))
    in_specs = [pl.BlockSpec((ts, d_model), lambda t: (jnp.minimum(t, n_tiles - 1), 0))]
    in_specs += [_resident(op.shape) for op in operands[1:]]
    scratch = [
        pltpu.VMEM((2, ts, d_model), f32),
        pltpu.VMEM((ts, d_model), bf16),
        pltpu.VMEM((ts, d_ff), bf16),
        pltpu.VMEM((ts, d_model), f32),
        pltpu.VMEM((ts, d_model), bf16),
        pltpu.VMEM((ts + SUBLANES, w_in.shape[1]), f32),
        pltpu.VMEM((ts, d_lru), f32),
        pltpu.VMEM((ts, d_lru), bf16),
        pltpu.VMEM((ts, 2 * d_lru), f32),
        pltpu.VMEM((ts + HALO_LONG, d_cm), f32),
        pltpu.VMEM((SUBLANES - 1, ts + HALO_LONG - SUBLANES, d_cm), f32),
        pltpu.VMEM((ts, d_lru + d_cm), bf16),
        pltpu.VMEM((SUBLANES, d_lru), f32),
    ]
    out = pl.pallas_call(
        functools.partial(_block_kernel, tiles_per_seq=seq // ts, final_norm=final_norm),
        grid=(n_tiles + 1,),
        in_specs=in_specs,
        out_specs=pl.BlockSpec((ts, d_model), lambda t: (jnp.maximum(t - 1, 0), 0)),
        out_shape=jax.ShapeDtypeStruct((n_tok, d_model), x.dtype),
        scratch_shapes=scratch,
        compiler_params=pltpu.CompilerParams(
            dimension_semantics=("arbitrary",),
            vmem_limit_bytes=VMEM_LIMIT_BYTES),
        name="block",
    )(*operands)
    return out.reshape(bsz, seq, d_model)


def kernel(x, norm_mix, w_in, conv4_w, conv4_b, gate_a_w, gate_a_b, gate_x_w, gate_x_b, lru_lambda, dw31_w, dw31_b, cm_ln_g, cm_ln_b, out_norm_lru, out_norm_cm, w_out, norm_ffn, w_gate, w_up, w_down, norm_final):
    depth = w_in.shape[0]
    for l in range(depth):
        x = _layer(x, norm_mix[l], w_in[l], conv4_w[l], conv4_b[l], gate_a_w[l], gate_a_b[l],
                   gate_x_w[l], gate_x_b[l], lru_lambda[l], dw31_w[l], dw31_b[l], cm_ln_g[l],
                   cm_ln_b[l], out_norm_lru[l], out_norm_cm[l], w_out[l], norm_ffn[l], w_gate[l],
                   w_up[l], w_down[l], norm_final, final_norm=(l == depth - 1))
    return x
```

```python
import functools
import math

import jax
import jax.numpy as jnp
from jax import lax
from jax.experimental import pallas as pl
from jax.experimental.pallas import tpu as pltpu

EPS = 1e-6
LRU_C = 8.0
SUBLANES = 8
LANES = 128
MXU_DIM = 256
CONV_SHORT = 4
CONV_LONG = 31
HALO_LONG = 32
TOK_TILE = 256
ROW_CHUNK = 32
FFN_COL_CHUNK = 256
EARLY_FFN_PIECES = 4
VMEM_LIMIT_BYTES = 56 * 1024 * 1024


def _sigmoid(v):
    return 0.5 + 0.5 * jnp.tanh(0.5 * v)


def _silu(v):
    hv = 0.5 * v
    return hv + hv * jnp.tanh(hv)


def _gelu_tanh(v):
    inner = math.sqrt(2.0 / math.pi) * (v + 0.044715 * (v * v * v))
    return v * (0.5 * (1.0 + jnp.tanh(inner)))


def _rms_scale(v, width):
    return lax.rsqrt(jnp.sum(v * v, axis=-1, keepdims=True) * (1.0 / width) + EPS)


def _tie(dst, src):
    if src is None:
        return dst
    dst3 = dst.reshape(-1, SUBLANES, LANES)
    never = jnp.abs(src) < -1.0
    return jnp.where(never[None], src[None], dst3).reshape(dst.shape)


def _spread(n_items, n_bins):
    return [n_items // n_bins + (1 if i < n_items % n_bins else 0) for i in range(n_bins)]


def _mixer_pieces(x_ref, x1_ref, nmix_ref, w_in_ref, c4w_ref, c4b_ref, wgate_ref, gab_ref, gxb_ref,
                  lam_ref, dww_ref, dwb_ref, lng_ref, lnb_ref, onl_ref, onc_ref, w_out_ref,
                  h_s, proj_s, c_s, cb_s, gates_s, ext_s, sh_s, y_s):
    ts, d_model = x_ref.shape
    d_lru = c4b_ref.shape[1]
    d_cm = dwb_ref.shape[1]
    rc = ROW_CHUNK
    groups = rc // SUBLANES
    first_tap = HALO_LONG - (CONV_LONG - 1)
    col_xl = slice(0, d_lru)
    col_gl = slice(d_lru, 2 * d_lru)
    col_vg = slice(2 * d_lru, 2 * d_lru + 2 * d_cm)

    def norm():
        nmix = nmix_ref[...]
        for r0 in range(0, ts, rc):
            xv = x_ref[pl.ds(r0, rc), :]
            h_s[pl.ds(r0, rc), :] = (xv * _rms_scale(xv, d_model) * nmix).astype(jnp.bfloat16)

    def in_proj(cols):
        proj_s[pl.ds(SUBLANES, ts), cols] = jnp.dot(h_s[...], w_in_ref[:, cols],
                                                    preferred_element_type=jnp.float32)

    def slab_cols(width):
        return [slice(ls, ls + LANES) for ls in range(0, width, LANES)]

    def conv4_glu(r0, after=None):
        for cols in slab_cols(d_lru):
            c4w = c4w_ref[:, cols]
            win = proj_s[pl.ds(r0, rc + SUBLANES), cols]
            win = _tie(win, after)
            acc = c4b_ref[:, cols] + c4w[CONV_SHORT - 1:CONV_SHORT, :] * win[SUBLANES:, :]
            for s in range(1, CONV_SHORT):
                tap = pltpu.roll(win, s, axis=0)[SUBLANES:, :]
                acc = acc + c4w[CONV_SHORT - 1 - s:CONV_SHORT - s, :] * tap
            c_s[pl.ds(r0, rc), cols] = acc
            cb_s[pl.ds(r0, rc), cols] = acc.astype(jnp.bfloat16)
        for cols in slab_cols(d_cm):
            v = proj_s[pl.ds(r0 + SUBLANES, rc), pl.ds(2 * d_lru + cols.start, LANES)]
            g = proj_s[pl.ds(r0 + SUBLANES, rc), pl.ds(2 * d_lru + d_cm + cols.start, LANES)]
            ext_s[pl.ds(r0 + HALO_LONG, rc), cols] = v * _sigmoid(g)

    def shift(r0, n_rows=rc):
        for cols in slab_cols(d_cm):
            win = ext_s[pl.ds(r0, n_rows + SUBLANES), cols]
            for r in range(1, SUBLANES):
                rolled = pltpu.roll(win, n_rows + SUBLANES - r, axis=0)
                sh_s[r - 1, pl.ds(r0, n_rows), cols] = rolled[0:n_rows, :]

    def gates():
        blk = wgate_ref.shape[1]
        for b in range(wgate_ref.shape[0]):
            res = jnp.dot(cb_s[:, b * blk:(b + 1) * blk], wgate_ref[b],
                          preferred_element_type=jnp.float32)
            gates_s[:, b * blk:(b + 1) * blk] = res[:, :blk]
            gates_s[:, d_lru + b * blk:d_lru + (b + 1) * blk] = res[:, blk:]

    def lru(r0, hprev, after=None):
        row_in_group = lax.broadcasted_iota(jnp.int32, (groups, SUBLANES, LANES), 1)
        hprev = [_tie(hp, after) for hp in hprev]
        ys, hnext = [], []
        for cols, hp in zip(slab_cols(d_lru), hprev):
            z = -lam_ref[:, cols]
            neg_c_softplus = -LRU_C * (jnp.maximum(z, 0.0) + jnp.log1p(jnp.exp(-jnp.abs(z))))
            cv = c_s[pl.ds(r0, rc), cols]
            r_gate = _sigmoid(gates_s[pl.ds(r0, rc), cols] + gab_ref[:, cols])
            i_gate = _sigmoid(gates_s[pl.ds(r0, rc), pl.ds(d_lru + cols.start, LANES)]
                              + gxb_ref[:, cols])
            log_a = neg_c_softplus * r_gate
            a = jnp.exp(log_a)
            b = jnp.sqrt(-jnp.tanh(log_a) * (a * a + 1.0)) * (i_gate * cv)
            a3 = a.reshape(groups, SUBLANES, LANES)
            b3 = b.reshape(groups, SUBLANES, LANES)
            for s in (1, 2, 4):
                keep = row_in_group >= s
                a_prev = jnp.where(keep, pltpu.roll(a3, s, axis=1), 1.0)
                b_prev = jnp.where(keep, pltpu.roll(b3, s, axis=1), 0.0)
                b3 = a3 * b_prev + b3
                a3 = a3 * a_prev
            hs = []
            for gi in range(groups):
                hg = b3[gi] + a3[gi] * hp
                hs.append(hg)
                hp = jnp.broadcast_to(hg[SUBLANES - 1:SUBLANES, :], (SUBLANES, LANES))
            hnext.append(hp)
            gl = proj_s[pl.ds(r0 + SUBLANES, rc), pl.ds(d_lru + cols.start, LANES)]
            ys.append(jnp.concatenate(hs, axis=0) * _gelu_tanh(gl))
        scale = lax.rsqrt(sum(jnp.sum(y * y, axis=-1, keepdims=True) for y in ys)
                          * (1.0 / d_lru) + EPS)
        for cols, y in zip(slab_cols(d_lru), ys):
            y_s[pl.ds(r0, rc), cols] = (y * scale * onl_ref[:, cols]).astype(jnp.bfloat16)
        return hnext

    def conv31(r0, after=None):
        ds = []
        for cols in slab_cols(d_cm):
            acc = jnp.broadcast_to(dwb_ref[:, cols], (rc, LANES)).reshape(groups, SUBLANES, LANES)
            for k in range(CONV_LONG):
                q, r = divmod(first_tap + k, SUBLANES)
                src = ext_s if r == 0 else sh_s.at[r - 1]
                tap = src[pl.ds(r0 + q * SUBLANES, rc), cols].reshape(groups, SUBLANES, LANES)
                wk = _tie(dww_ref[k * SUBLANES:(k + 1) * SUBLANES, cols], after)
                acc = acc + wk[None] * tap
            ds.append(acc.reshape(rc, LANES))
        mu = sum(jnp.sum(d, axis=-1, keepdims=True) for d in ds) * (1.0 / d_cm)
        dcs = [d - mu for d in ds]
        var = sum(jnp.sum(dc * dc, axis=-1, keepdims=True) for dc in dcs) * (1.0 / d_cm)
        inv = lax.rsqrt(var + EPS)
        ys = []
        for cols, dc in zip(slab_cols(d_cm), dcs):
            ln = dc * inv * lng_ref[:, cols] + lnb_ref[:, cols]
            ys.append(ln * _sigmoid(ln))
        scale = lax.rsqrt(sum(jnp.sum(y * y, axis=-1, keepdims=True) for y in ys)
                          * (1.0 / d_cm) + EPS)
        for cols, y in zip(slab_cols(d_cm), ys):
            y_s[pl.ds(r0, rc), pl.ds(d_lru + cols.start, LANES)] = (
                y * scale * onc_ref[:, cols]).astype(jnp.bfloat16)

    def out_proj():
        x1_ref[...] = x_ref[...] + jnp.dot(y_s[...], w_out_ref[...],
                                           preferred_element_type=jnp.float32)
        proj_s[0:SUBLANES, col_xl] = proj_s[ts:ts + SUBLANES, col_xl]
        ext_s[0:HALO_LONG, 0:d_cm] = ext_s[ts:ts + HALO_LONG, 0:d_cm]

    return dict(norm=norm, in_proj=in_proj, conv4_glu=conv4_glu, shift=shift, gates=gates,
                lru=lru, conv31=conv31, out_proj=out_proj,
                col_xl=col_xl, col_gl=col_gl, col_vg=col_vg)


def _ffn_pieces(x_ref, o_ref, nffn_ref, wg_ref, wu_ref, wd_ref, nfin_ref, hf_s, f_s, yf_s,
                *, final_norm):
    ts, d_model = x_ref.shape

    def norm():
        xv = x_ref[...]
        hf_s[...] = (xv * _rms_scale(xv, d_model) * nffn_ref[...]).astype(jnp.bfloat16)

    def gate_up(c):
        cols = slice(c * FFN_COL_CHUNK, (c + 1) * FFN_COL_CHUNK)
        h = hf_s[...]
        g = jnp.dot(h, wg_ref[:, cols], preferred_element_type=jnp.float32)
        u = jnp.dot(h, wu_ref[:, cols], preferred_element_type=jnp.float32)
        f = _silu(g) * u
        f_s[:, cols] = f.astype(jnp.bfloat16)
        return f[-SUBLANES:, -LANES:]

    def down(n):
        cols = slice(n * FFN_COL_CHUNK, (n + 1) * FFN_COL_CHUNK)
        y = x_ref[:, cols] + jnp.dot(f_s[...], wd_ref[:, cols],
                                     preferred_element_type=jnp.float32)
        yf_s[:, cols] = y
        return y[-SUBLANES:, -LANES:]

    def finish():
        y = yf_s[...]
        if final_norm:
            y = y * _rms_scale(y, d_model) * nfin_ref[...]
        o_ref[...] = y

    return dict(norm=norm, gate_up=gate_up, down=down, finish=finish)


def _block_kernel(x_ref, nmix_ref, w_in_ref, c4w_ref, c4b_ref, wgate_ref, gab_ref, gxb_ref,
                  lam_ref, dww_ref, dwb_ref, lng_ref, lnb_ref, onl_ref, onc_ref, w_out_ref,
                  nffn_ref, wg_ref, wu_ref, wd_ref, nfin_ref,
                  o_ref,
                  x1_s, hf_s, f_s, yf_s, h_s, proj_s, c_s, cb_s, gates_s, ext_s, sh_s, y_s, hc_s,
                  *, tiles_per_seq, final_norm):
    t = pl.program_id(0)
    ts, d_model = x_ref.shape
    d_lru = c4b_ref.shape[1]
    d_cm = dwb_ref.shape[1]
    d_ff = wd_ref.shape[0]
    rc = ROW_CHUNK
    slot = lax.rem(t, 2)

    @pl.when(t == 0)
    def _():
        x1_s[1] = jnp.zeros(x1_s.shape[1:], jnp.float32)

    @pl.when(lax.rem(t, tiles_per_seq) == 0)
    def _():
        proj_s[0:SUBLANES, 0:d_lru] = jnp.zeros((SUBLANES, d_lru), jnp.float32)
        ext_s[0:HALO_LONG, 0:d_cm] = jnp.zeros((HALO_LONG, d_cm), jnp.float32)
        hc_s[...] = jnp.zeros_like(hc_s)

    mix = _mixer_pieces(x_ref, x1_s.at[slot], nmix_ref, w_in_ref, c4w_ref, c4b_ref, wgate_ref,
                        gab_ref, gxb_ref, lam_ref, dww_ref, dwb_ref, lng_ref, lnb_ref, onl_ref,
                        onc_ref, w_out_ref, h_s, proj_s, c_s, cb_s, gates_s, ext_s, sh_s, y_s)
    ffn = _ffn_pieces(x1_s.at[1 - slot], o_ref, nffn_ref, wg_ref, wu_ref, wd_ref, nfin_ref,
                      hf_s, f_s, yf_s, final_norm=final_norm)

    row_chunks = list(range(0, ts, rc))
    n_gate_up = d_ff // FFN_COL_CHUNK
    n_down = d_model // FFN_COL_CHUNK
    ffn_matmuls = iter([functools.partial(ffn["gate_up"], c) for c in range(n_gate_up)]
                       + [functools.partial(ffn["down"], n) for n in range(n_down)])

    def ffn_matmul(wanted=True):
        piece = next(ffn_matmuls, None) if wanted else None
        return None if piece is None else piece()

    ffn["norm"]()
    mix["norm"]()
    mix["in_proj"](mix["col_xl"])
    ffn_matmul()
    mix["in_proj"](mix["col_vg"])
    ffn_matmul()
    mix["in_proj"](mix["col_gl"])
    for i, r0 in enumerate(row_chunks):
        mix["conv4_glu"](r0)
        if i > 0:
            mix["shift"](row_chunks[i - 1])
    mix["shift"](row_chunks[-1])
    mix["shift"](ts, HALO_LONG - SUBLANES)
    for _ in range(EARLY_FFN_PIECES - 2):
        ffn_matmul()
    mix["gates"]()
    hprev = [hc_s[:, ls:ls + LANES] for ls in range(0, d_lru, LANES)]
    for i, r0 in enumerate(row_chunks):
        mix["conv31"](r0, after=ffn_matmul())
        hprev = mix["lru"](r0, hprev, after=ffn_matmul(i % 2 == 0))
    hc_s[...] = jnp.concatenate(hprev, axis=1)
    while ffn_matmul() is not None:
        pass
    mix["out_proj"]()
    ffn["finish"]()


def _resident(shape):
    zeros = (0,) * len(shape)
    return pl.BlockSpec(shape, lambda *_: zeros, pipeline_mode=pl.Buffered(1))


def _block_diag(w, block):
    heads, d, _ = w.shape
    per_block = block // d
    eye = jnp.eye(per_block, dtype=w.dtype)
    w = w.reshape(heads // per_block, per_block, d, d)
    return jnp.einsum('nhij,hg->nhigj', w, eye).reshape(heads // per_block, block, block)


def _layer(x, norm_mix, w_in, conv4_w, conv4_b, gate_a_w, gate_a_b, gate_x_w, gate_x_b,
           lru_lambda, dw31_w, dw31_b, cm_ln_g, cm_ln_b, out_norm_lru, out_norm_cm, w_out,
           norm_ffn, w_gate, w_up, w_down, norm_final, final_norm):
    bsz, seq, d_model = x.shape
    d_lru = conv4_b.shape[0]
    d_cm = dw31_b.shape[0]
    d_ff = w_gate.shape[1]
    ts = TOK_TILE
    n_tok = bsz * seq
    n_tiles = n_tok // ts
    assert seq % ts == 0 and ts % ROW_CHUNK == 0
    assert d_ff % FFN_COL_CHUNK == 0 and d_model % FFN_COL_CHUNK == 0
    assert conv4_w.shape[0] == CONV_SHORT and dw31_w.shape[0] == CONV_LONG
    bf16 = jnp.bfloat16
    f32 = jnp.float32
    row = lambda v: v.reshape(1, -1)
    assert MXU_DIM % gate_a_w.shape[1] == 0 and d_lru % MXU_DIM == 0
    w_gates = jnp.concatenate([_block_diag(gate_a_w, MXU_DIM), _block_diag(gate_x_w, MXU_DIM)],
                              axis=2).astype(bf16)
    dww_rows = jnp.repeat(dw31_w, SUBLANES, axis=0)
    operands = (
        x.reshape(n_tok, d_model), row(norm_mix), w_in.astype(bf16), conv4_w, row(conv4_b),
        w_gates, row(gate_a_b), row(gate_x_b), row(lru_lambda), dww_rows, row(dw31_b),
        row(cm_ln_g), row(cm_ln_b), row(out_norm_lru), row(out_norm_cm), w_out.astype(bf16),
        row(norm_ffn), w_gate.astype(bf16), w_up.astype(bf16), w_down.astype(bf16),
        row(norm_final))
    in_specs = [pl.BlockSpec((ts, d_model), lambda t: (jnp.minimum(t, n_tiles - 1), 0))]
    in_specs += [_resident(op.shape) for op in operands[1:]]
    scratch = [
        pltpu.VMEM((2, ts, d_model), f32),
        pltpu.VMEM((ts, d_model), bf16),
        pltpu.VMEM((ts, d_ff), bf16),
        pltpu.VMEM((ts, d_model), f32),
        pltpu.VMEM((ts, d_model), bf16),
        pltpu.VMEM((ts + SUBLANES, w_in.shape[1]), f32),
        pltpu.VMEM((ts, d_lru), f32),
        pltpu.VMEM((ts, d_lru), bf16),
        pltpu.VMEM((ts, 2 * d_lru), f32),
        pltpu.VMEM((ts + HALO_LONG, d_cm + LANES), f32),
        pltpu.VMEM((SUBLANES - 1, ts + HALO_LONG - SUBLANES, d_cm + LANES), f32),
        pltpu.VMEM((ts, d_lru + d_cm), bf16),
        pltpu.VMEM((SUBLANES, d_lru), f32),
    ]
    out = pl.pallas_call(
        functools.partial(_block_kernel, tiles_per_seq=seq // ts, final_norm=final_norm),
        grid=(n_tiles + 1,),
        in_specs=in_specs,
        out_specs=pl.BlockSpec((ts, d_model), lambda t: (jnp.maximum(t - 1, 0), 0)),
        out_shape=jax.ShapeDtypeStruct((n_tok, d_model), x.dtype),
        scratch_shapes=scratch,
        compiler_params=pltpu.CompilerParams(
            dimension_semantics=("arbitrary",),
            vmem_limit_bytes=VMEM_LIMIT_BYTES),
        name="block",
    )(*operands)
    return out.reshape(bsz, seq, d_model)


def kernel(x, norm_mix, w_in, conv4_w, conv4_b, gate_a_w, gate_a_b, gate_x_w, gate_x_b, lru_lambda, dw31_w, dw31_b, cm_ln_g, cm_ln_b, out_norm_lru, out_norm_cm, w_out, norm_ffn, w_gate, w_up, w_down, norm_final):
    depth = w_in.shape[0]
    for l in range(depth):
        x = _layer(x, norm_mix[l], w_in[l], conv4_w[l], conv4_b[l], gate_a_w[l], gate_a_b[l],
                   gate_x_w[l], gate_x_b[l], lru_lambda[l], dw31_w[l], dw31_b[l], cm_ln_g[l],
                   cm_ln_b[l], out_norm_lru[l], out_norm_cm[l], w_out[l], norm_ffn[l], w_gate[l],
                   w_up[l], w_down[l], norm_final, final_norm=(l == depth - 1))
    return x
```

```python
import functools
import math

import jax
import jax.numpy as jnp
from jax import lax
from jax.experimental import pallas as pl
from jax.experimental.pallas import tpu as pltpu

EPS = 1e-6
LRU_C = 8.0
SUBLANES = 8
LANES = 128
SLAB = 512
MXU_DIM = 256
CONV_SHORT = 4
CONV_LONG = 31
HALO_LONG = 32
TOK_TILE = 256
ROW_CHUNK = 16
FFN_COL_CHUNK = 256
EARLY_FFN_PIECES = 4
VMEM_LIMIT_BYTES = 56 * 1024 * 1024


def _sigmoid(v):
    return 0.5 + 0.5 * jnp.tanh(0.5 * v)


def _silu(v):
    hv = 0.5 * v
    return hv + hv * jnp.tanh(hv)


def _gelu_tanh(v):
    inner = math.sqrt(2.0 / math.pi) * (v + 0.044715 * (v * v * v))
    return v * (0.5 * (1.0 + jnp.tanh(inner)))


def _rms_scale(v, width):
    return lax.rsqrt(jnp.sum(v * v, axis=-1, keepdims=True) * (1.0 / width) + EPS)


def _tie(dst, src):
    if src is None:
        return dst
    width = dst.shape[-1]
    src = jnp.tile(src, (1, width // LANES))
    dst3 = dst.reshape(-1, SUBLANES, width)
    never = jnp.abs(src) < -1.0
    return jnp.where(never[None], src[None], dst3).reshape(dst.shape)


def _mixer_pieces(x_ref, x1_ref, nmix_ref, w_in_ref, c4w_ref, c4b_ref, wgate_ref, gab_ref, gxb_ref,
                  lam_ref, dww_ref, dwb_ref, lng_ref, lnb_ref, onl_ref, onc_ref, w_out_ref,
                  h_s, proj_s, c_s, cb_s, gates_s, ext_s, sh_s, y_s):
    ts, d_model = x_ref.shape
    d_lru = c4b_ref.shape[1]
    d_cm = dwb_ref.shape[1]
    rc = ROW_CHUNK
    groups = rc // SUBLANES
    first_tap = HALO_LONG - (CONV_LONG - 1)
    col_xl = slice(0, d_lru)
    col_gl = slice(d_lru, 2 * d_lru)
    col_vg = slice(2 * d_lru, 2 * d_lru + 2 * d_cm)

    def norm():
        nmix = nmix_ref[...]
        for r0 in range(0, ts, rc):
            xv = x_ref[pl.ds(r0, rc), :]
            h_s[pl.ds(r0, rc), :] = (xv * _rms_scale(xv, d_model) * nmix).astype(jnp.bfloat16)

    def in_proj(cols):
        proj_s[pl.ds(SUBLANES, ts), cols] = jnp.dot(h_s[...], w_in_ref[:, cols],
                                                    preferred_element_type=jnp.float32)

    def slab_cols(width):
        return [slice(ls, ls + SLAB) for ls in range(0, width, SLAB)]

    def conv4_glu(r0):
        for cols in slab_cols(d_lru):
            c4w = c4w_ref[:, cols]
            win = proj_s[pl.ds(r0, rc + SUBLANES), cols]
            acc = c4b_ref[:, cols] + c4w[CONV_SHORT - 1:CONV_SHORT, :] * win[SUBLANES:, :]
            for s in range(1, CONV_SHORT):
                tap = pltpu.roll(win, s, axis=0)[SUBLANES:, :]
                acc = acc + c4w[CONV_SHORT - 1 - s:CONV_SHORT - s, :] * tap
            c_s[pl.ds(r0, rc), cols] = acc
            cb_s[pl.ds(r0, rc), cols] = acc.astype(jnp.bfloat16)
        for cols in slab_cols(d_cm):
            v = proj_s[pl.ds(r0 + SUBLANES, rc), pl.ds(2 * d_lru + cols.start, SLAB)]
            g = proj_s[pl.ds(r0 + SUBLANES, rc), pl.ds(2 * d_lru + d_cm + cols.start, SLAB)]
            ext_s[pl.ds(r0 + HALO_LONG, rc), cols] = v * _sigmoid(g)

    def shift(r0, n_rows=rc):
        for cols in slab_cols(d_cm):
            win = ext_s[pl.ds(r0, n_rows + SUBLANES), cols]
            for r in range(1, SUBLANES):
                rolled = pltpu.roll(win, n_rows + SUBLANES - r, axis=0)
                sh_s[r - 1, pl.ds(r0, n_rows), cols] = rolled[0:n_rows, :]

    def gates():
        blk = wgate_ref.shape[1]
        for b in range(wgate_ref.shape[0]):
            res = jnp.dot(cb_s[:, b * blk:(b + 1) * blk], wgate_ref[b],
                          preferred_element_type=jnp.float32)
            gates_s[:, b * blk:(b + 1) * blk] = res[:, :blk]
            gates_s[:, d_lru + b * blk:d_lru + (b + 1) * blk] = res[:, blk:]

    def lru(r0, hprev, after=None):
        row_in_group = lax.broadcasted_iota(jnp.int32, (groups, SUBLANES, SLAB), 1)
        hprev = [_tie(hp, after) for hp in hprev]
        ys, hnext = [], []
        for cols, hp in zip(slab_cols(d_lru), hprev):
            z = -lam_ref[:, cols]
            neg_c_softplus = -LRU_C * (jnp.maximum(z, 0.0) + jnp.log1p(jnp.exp(-jnp.abs(z))))
            cv = c_s[pl.ds(r0, rc), cols]
            r_gate = _sigmoid(gates_s[pl.ds(r0, rc), cols] + gab_ref[:, cols])
            i_gate = _sigmoid(gates_s[pl.ds(r0, rc), pl.ds(d_lru + cols.start, SLAB)]
                              + gxb_ref[:, cols])
            log_a = neg_c_softplus * r_gate
            a = jnp.exp(log_a)
            b = jnp.sqrt(-jnp.tanh(log_a) * (a * a + 1.0)) * (i_gate * cv)
            a3 = a.reshape(groups, SUBLANES, SLAB)
            b3 = b.reshape(groups, SUBLANES, SLAB)
            for s in (1, 2, 4):
                keep = row_in_group >= s
                a_prev = jnp.where(keep, pltpu.roll(a3, s, axis=1), 1.0)
                b_prev = jnp.where(keep, pltpu.roll(b3, s, axis=1), 0.0)
                b3 = a3 * b_prev + b3
                a3 = a3 * a_prev
            hs = []
            for gi in range(groups):
                hg = b3[gi] + a3[gi] * hp
                hs.append(hg)
                hp = jnp.broadcast_to(hg[SUBLANES - 1:SUBLANES, :], (SUBLANES, SLAB))
            hnext.append(hp)
            gl = proj_s[pl.ds(r0 + SUBLANES, rc), pl.ds(d_lru + cols.start, SLAB)]
            ys.append(jnp.concatenate(hs, axis=0) * _gelu_tanh(gl))
        scale = lax.rsqrt(sum(jnp.sum(y * y, axis=-1, keepdims=True) for y in ys)
                          * (1.0 / d_lru) + EPS)
        for cols, y in zip(slab_cols(d_lru), ys):
            y_s[pl.ds(r0, rc), cols] = (y * scale * onl_ref[:, cols]).astype(jnp.bfloat16)
        return hnext

    def conv31(r0, after=None):
        ds = []
        for cols in slab_cols(d_cm):
            acc = jnp.broadcast_to(dwb_ref[:, cols], (rc, SLAB)).reshape(groups, SUBLANES, SLAB)
            for k in range(CONV_LONG):
                q, r = divmod(first_tap + k, SUBLANES)
                src = ext_s if r == 0 else sh_s.at[r - 1]
                tap = src[pl.ds(r0 + q * SUBLANES, rc), cols].reshape(groups, SUBLANES, SLAB)
                wk = _tie(dww_ref[k * SUBLANES:(k + 1) * SUBLANES, cols], after)
                acc = acc + wk[None] * tap
            ds.append(acc.reshape(rc, SLAB))
        mu = sum(jnp.sum(d, axis=-1, keepdims=True) for d in ds) * (1.0 / d_cm)
        dcs = [d - mu for d in ds]
        var = sum(jnp.sum(dc * dc, axis=-1, keepdims=True) for dc in dcs) * (1.0 / d_cm)
        inv = lax.rsqrt(var + EPS)
        ys = []
        for cols, dc in zip(slab_cols(d_cm), dcs):
            ln = dc * inv * lng_ref[:, cols] + lnb_ref[:, cols]
            ys.append(ln * _sigmoid(ln))
        scale = lax.rsqrt(sum(jnp.sum(y * y, axis=-1, keepdims=True) for y in ys)
                          * (1.0 / d_cm) + EPS)
        for cols, y in zip(slab_cols(d_cm), ys):
            y_s[pl.ds(r0, rc), pl.ds(d_lru + cols.start, SLAB)] = (
                y * scale * onc_ref[:, cols]).astype(jnp.bfloat16)

    def out_proj():
        x1_ref[...] = x_ref[...] + jnp.dot(y_s[...], w_out_ref[...],
                                           preferred_element_type=jnp.float32)
        proj_s[0:SUBLANES, col_xl] = proj_s[ts:ts + SUBLANES, col_xl]
        ext_s[0:HALO_LONG, 0:d_cm] = ext_s[ts:ts + HALO_LONG, 0:d_cm]

    return dict(norm=norm, in_proj=in_proj, conv4_glu=conv4_glu, shift=shift, gates=gates,
                lru=lru, conv31=conv31, out_proj=out_proj,
                col_xl=col_xl, col_gl=col_gl, col_vg=col_vg)


def _ffn_pieces(x_ref, o_ref, nffn_ref, wg_ref, wu_ref, wd_ref, nfin_ref, hf_s, f_s, yf_s,
                *, final_norm):
    ts, d_model = x_ref.shape

    def norm():
        xv = x_ref[...]
        hf_s[...] = (xv * _rms_scale(xv, d_model) * nffn_ref[...]).astype(jnp.bfloat16)

    def gate_up(c):
        cols = slice(c * FFN_COL_CHUNK, (c + 1) * FFN_COL_CHUNK)
        h = hf_s[...]
        g = jnp.dot(h, wg_ref[:, cols], preferred_element_type=jnp.float32)
        u = jnp.dot(h, wu_ref[:, cols], preferred_element_type=jnp.float32)
        f = _silu(g) * u
        f_s[:, cols] = f.astype(jnp.bfloat16)
        return f[-SUBLANES:, -LANES:]

    def down(n):
        cols = slice(n * FFN_COL_CHUNK, (n + 1) * FFN_COL_CHUNK)
        y = x_ref[:, cols] + jnp.dot(f_s[...], wd_ref[:, cols],
                                     preferred_element_type=jnp.float32)
        yf_s[:, cols] = y
        return y[-SUBLANES:, -LANES:]

    def finish():
        y = yf_s[...]
        if final_norm:
            y = y * _rms_scale(y, d_model) * nfin_ref[...]
        o_ref[...] = y

    return dict(norm=norm, gate_up=gate_up, down=down, finish=finish)


def _block_kernel(x_ref, nmix_ref, w_in_ref, c4w_ref, c4b_ref, wgate_ref, gab_ref, gxb_ref,
                  lam_ref, dww_ref, dwb_ref, lng_ref, lnb_ref, onl_ref, onc_ref, w_out_ref,
                  nffn_ref, wg_ref, wu_ref, wd_ref, nfin_ref,
                  o_ref,
                  x1_s, hf_s, f_s, yf_s, h_s, proj_s, c_s, cb_s, gates_s, ext_s, sh_s, y_s, hc_s,
                  *, tiles_per_seq, final_norm):
    t = pl.program_id(0)
    ts, d_model = x_ref.shape
    d_lru = c4b_ref.shape[1]
    d_cm = dwb_ref.shape[1]
    d_ff = wd_ref.shape[0]
    rc = ROW_CHUNK
    slot = lax.rem(t, 2)

    @pl.when(t == 0)
    def _():
        x1_s[1] = jnp.zeros(x1_s.shape[1:], jnp.float32)

    @pl.when(lax.rem(t, tiles_per_seq) == 0)
    def _():
        proj_s[0:SUBLANES, 0:d_lru] = jnp.zeros((SUBLANES, d_lru), jnp.float32)
        ext_s[0:HALO_LONG, 0:d_cm] = jnp.zeros((HALO_LONG, d_cm), jnp.float32)
        hc_s[...] = jnp.zeros_like(hc_s)

    mix = _mixer_pieces(x_ref, x1_s.at[slot], nmix_ref, w_in_ref, c4w_ref, c4b_ref, wgate_ref,
                        gab_ref, gxb_ref, lam_ref, dww_ref, dwb_ref, lng_ref, lnb_ref, onl_ref,
                        onc_ref, w_out_ref, h_s, proj_s, c_s, cb_s, gates_s, ext_s, sh_s, y_s)
    ffn = _ffn_pieces(x1_s.at[1 - slot], o_ref, nffn_ref, wg_ref, wu_ref, wd_ref, nfin_ref,
                      hf_s, f_s, yf_s, final_norm=final_norm)

    row_chunks = list(range(0, ts, rc))
    n_gate_up = d_ff // FFN_COL_CHUNK
    n_down = d_model // FFN_COL_CHUNK
    ffn_matmuls = iter([functools.partial(ffn["gate_up"], c) for c in range(n_gate_up)]
                       + [functools.partial(ffn["down"], n) for n in range(n_down)])

    def ffn_matmul(wanted=True):
        piece = next(ffn_matmuls, None) if wanted else None
        return None if piece is None else piece()

    ffn["norm"]()
    mix["norm"]()
    mix["in_proj"](mix["col_xl"])
    ffn_matmul()
    mix["in_proj"](mix["col_vg"])
    ffn_matmul()
    mix["in_proj"](mix["col_gl"])
    for i, r0 in enumerate(row_chunks):
        mix["conv4_glu"](r0)
        if i > 0:
            mix["shift"](row_chunks[i - 1])
    mix["shift"](row_chunks[-1])
    mix["shift"](ts, HALO_LONG - SUBLANES)
    for _ in range(EARLY_FFN_PIECES - 2):
        ffn_matmul()
    mix["gates"]()
    hprev = [hc_s[:, ls:ls + SLAB] for ls in range(0, d_lru, SLAB)]
    for i, r0 in enumerate(row_chunks):
        mix["conv31"](r0, after=ffn_matmul(i % 4 == 0))
        hprev = mix["lru"](r0, hprev, after=ffn_matmul())
    hc_s[...] = jnp.concatenate(hprev, axis=1)
    while ffn_matmul() is not None:
        pass
    mix["out_proj"]()
    ffn["finish"]()


def _resident(shape):
    zeros = (0,) * len(shape)
    return pl.BlockSpec(shape, lambda *_: zeros, pipeline_mode=pl.Buffered(1))


def _block_diag(w, block):
    heads, d, _ = w.shape
    per_block = block // d
    eye = jnp.eye(per_block, dtype=w.dtype)
    w = w.reshape(heads // per_block, per_block, d, d)
    return jnp.einsum('nhij,hg->nhigj', w, eye).reshape(heads // per_block, block, block)


def _layer(x, norm_mix, w_in, conv4_w, conv4_b, gate_a_w, gate_a_b, gate_x_w, gate_x_b,
           lru_lambda, dw31_w, dw31_b, cm_ln_g, cm_ln_b, out_norm_lru, out_norm_cm, w_out,
           norm_ffn, w_gate, w_up, w_down, norm_final, final_norm):
    bsz, seq, d_model = x.shape
    d_lru = conv4_b.shape[0]
    d_cm = dw31_b.shape[0]
    d_ff = w_gate.shape[1]
    ts = TOK_TILE
    n_tok = bsz * seq
    n_tiles = n_tok // ts
    assert seq % ts == 0 and ts % ROW_CHUNK == 0
    assert d_ff % FFN_COL_CHUNK == 0 and d_model % FFN_COL_CHUNK == 0
    assert d_lru % SLAB == 0 and d_cm % SLAB == 0
    assert conv4_w.shape[0] == CONV_SHORT and dw31_w.shape[0] == CONV_LONG
    bf16 = jnp.bfloat16
    f32 = jnp.float32
    row = lambda v: v.reshape(1, -1)
    assert MXU_DIM % gate_a_w.shape[1] == 0 and d_lru % MXU_DIM == 0
    w_gates = jnp.concatenate([_block_diag(gate_a_w, MXU_DIM), _block_diag(gate_x_w, MXU_DIM)],
                              axis=2).astype(bf16)
    dww_rows = jnp.repeat(dw31_w, SUBLANES, axis=0)
    operands = (
        x.reshape(n_tok, d_model), row(norm_mix), w_in.astype(bf16), conv4_w, row(conv4_b),
        w_gates, row(gate_a_b), row(gate_x_b), row(lru_lambda), dww_rows, row(dw31_b),
        row(cm_ln_g), row(cm_ln_b), row(out_norm_lru), row(out_norm_cm), w_out.astype(bf16),
        row(norm_ffn), w_gate.astype(bf16), w_up.astype(bf16), w_down.astype(bf16),
        row(norm_final))
    in_specs = [pl.BlockSpec((ts, d_model), lambda t: (jnp.minimum(t, n_tiles - 1), 0))]
    in_specs += [_resident(op.shape) for op in operands[1:]]
    scratch = [
        pltpu.VMEM((2, ts, d_model), f32),
        pltpu.VMEM((ts, d_model), bf16),
        pltpu.VMEM((ts, d_ff), bf16),
        pltpu.VMEM((ts, d_model), f32),
        pltpu.VMEM((ts, d_model), bf16),
        pltpu.VMEM((ts + SUBLANES, w_in.shape[1]), f32),
        pltpu.VMEM((ts, d_lru), f32),
        pltpu.VMEM((ts, d_lru), bf16),
        pltpu.VMEM((ts, 2 * d_lru), f32),
        pltpu.VMEM((ts + HALO_LONG, d_cm + LANES), f32),
        pltpu.VMEM((SUBLANES - 1, ts + HALO_LONG - SUBLANES, d_cm + LANES), f32),
        pltpu.VMEM((ts, d_lru + d_cm), bf16),
        pltpu.VMEM((SUBLANES, d_lru), f32),
    ]
    out = pl.pallas_call(
        functools.partial(_block_kernel, tiles_per_seq=seq // ts, final_norm=final_norm),
        grid=(n_tiles + 1,),
        in_specs=in_specs,
        out_specs=pl.BlockSpec((ts, d_model), lambda t: (jnp.maximum(t - 1, 0), 0)),
        out_shape=jax.ShapeDtypeStruct((n_tok, d_model), x.dtype),
        scratch_shapes=scratch,
        compiler_params=pltpu.CompilerParams(
            dimension_semantics=("arbitrary",),
            vmem_limit_bytes=VMEM_LIMIT_BYTES),
        name="block",
    )(*operands)
    return out.reshape(bsz, seq, d_model)


def kernel(x, norm_mix, w_in, conv4_w, conv4_b, gate_a_w, gate_a_b, gate_x_w, gate_x_b, lru_lambda, dw31_w, dw31_b, cm_ln_g, cm_ln_b, out_norm_lru, out_norm_cm, w_out, norm_ffn, w_gate, w_up, w_down, norm_final):
    depth = w_in.shape[0]
    for l in range(depth):
        x = _layer(x, norm_mix[l], w_in[l], conv4_w[l], conv4_b[l], gate_a_w[l], gate_a_b[l],
                   gate_x_w[l], gate_x_b[l], lru_lambda[l], dw31_w[l], dw31_b[l], cm_ln_g[l],
                   cm_ln_b[l], out_norm_lru[l], out_norm_cm[l], w_out[l], norm_ffn[l], w_gate[l],
                   w_up[l], w_down[l], norm_final, final_norm=(l == depth - 1))
    return x
```

```python
import functools
import math

import jax
import jax.numpy as jnp
from jax import lax
from jax.experimental import pallas as pl
from jax.experimental.pallas import tpu as pltpu

EPS = 1e-6
LRU_C = 8.0
SUBLANES = 8
LANES = 128
SLAB = 512
MXU_DIM = 256
CONV_SHORT = 4
CONV_LONG = 31
HALO_LONG = 32
TOK_TILE = 512
FFN_SUB_ROWS = 256
ROW_CHUNK = 16
FFN_COL_CHUNK = 256
EARLY_FFN_PIECES = 4
VMEM_LIMIT_BYTES = 56 * 1024 * 1024


def _sigmoid(v):
    return 0.5 + 0.5 * jnp.tanh(0.5 * v)


def _silu(v):
    hv = 0.5 * v
    return hv + hv * jnp.tanh(hv)


def _gelu_tanh(v):
    inner = math.sqrt(2.0 / math.pi) * (v + 0.044715 * (v * v * v))
    return v * (0.5 * (1.0 + jnp.tanh(inner)))


def _rms_scale(v, width):
    return lax.rsqrt(jnp.sum(v * v, axis=-1, keepdims=True) * (1.0 / width) + EPS)


def _tie(dst, src):
    if src is None:
        return dst
    width = dst.shape[-1]
    src = jnp.tile(src, (1, width // LANES))
    dst3 = dst.reshape(-1, SUBLANES, width)
    never = jnp.abs(src) < -1.0
    return jnp.where(never[None], src[None], dst3).reshape(dst.shape)


def _mixer_pieces(x_ref, x1_ref, nmix_ref, w_in_ref, c4w_ref, c4b_ref, wgate_ref, gab_ref, gxb_ref,
                  lam_ref, dww_ref, dwb_ref, lng_ref, lnb_ref, onl_ref, onc_ref, w_out_ref,
                  h_s, proj_s, c_s, cb_s, gates_s, ext_s, sh_s, y_s):
    ts, d_model = x_ref.shape
    d_lru = c4b_ref.shape[1]
    d_cm = dwb_ref.shape[1]
    rc = ROW_CHUNK
    groups = rc // SUBLANES
    first_tap = HALO_LONG - (CONV_LONG - 1)
    col_xl = slice(0, d_lru)
    col_gl = slice(d_lru, 2 * d_lru)
    col_vg = slice(2 * d_lru, 2 * d_lru + 2 * d_cm)

    def norm():
        nmix = nmix_ref[...]
        for r0 in range(0, ts, rc):
            xv = x_ref[pl.ds(r0, rc), :]
            h_s[pl.ds(r0, rc), :] = (xv * _rms_scale(xv, d_model) * nmix).astype(jnp.bfloat16)

    def in_proj(cols):
        proj_s[pl.ds(SUBLANES, ts), cols] = jnp.dot(h_s[...], w_in_ref[:, cols],
                                                    preferred_element_type=jnp.float32)

    def slab_cols(width):
        return [slice(ls, ls + SLAB) for ls in range(0, width, SLAB)]

    def conv4_glu(r0):
        for cols in slab_cols(d_lru):
            c4w = c4w_ref[:, cols]
            win = proj_s[pl.ds(r0, rc + SUBLANES), cols]
            acc = c4b_ref[:, cols] + c4w[CONV_SHORT - 1:CONV_SHORT, :] * win[SUBLANES:, :]
            for s in range(1, CONV_SHORT):
                tap = pltpu.roll(win, s, axis=0)[SUBLANES:, :]
                acc = acc + c4w[CONV_SHORT - 1 - s:CONV_SHORT - s, :] * tap
            c_s[pl.ds(r0, rc), cols] = acc
            cb_s[pl.ds(r0, rc), cols] = acc.astype(jnp.bfloat16)
        for cols in slab_cols(d_cm):
            v = proj_s[pl.ds(r0 + SUBLANES, rc), pl.ds(2 * d_lru + cols.start, SLAB)]
            g = proj_s[pl.ds(r0 + SUBLANES, rc), pl.ds(2 * d_lru + d_cm + cols.start, SLAB)]
            ext_s[pl.ds(r0 + HALO_LONG, rc), cols] = v * _sigmoid(g)

    def shift(r0, n_rows=rc):
        for cols in slab_cols(d_cm):
            win = ext_s[pl.ds(r0, n_rows + SUBLANES), cols]
            for r in range(1, SUBLANES):
                rolled = pltpu.roll(win, n_rows + SUBLANES - r, axis=0)
                sh_s[r - 1, pl.ds(r0, n_rows), cols] = rolled[0:n_rows, :]

    def gates():
        blk = wgate_ref.shape[1]
        for b in range(wgate_ref.shape[0]):
            res = jnp.dot(cb_s[:, b * blk:(b + 1) * blk], wgate_ref[b],
                          preferred_element_type=jnp.float32)
            gates_s[:, b * blk:(b + 1) * blk] = res[:, :blk]
            gates_s[:, d_lru + b * blk:d_lru + (b + 1) * blk] = res[:, blk:]

    def lru(r0, hprev, after=None):
        row_in_group = lax.broadcasted_iota(jnp.int32, (groups, SUBLANES, SLAB), 1)
        hprev = [_tie(hp, after) for hp in hprev]
        ys, hnext = [], []
        for cols, hp in zip(slab_cols(d_lru), hprev):
            z = -lam_ref[:, cols]
            neg_c_softplus = -LRU_C * (jnp.maximum(z, 0.0) + jnp.log1p(jnp.exp(-jnp.abs(z))))
            cv = c_s[pl.ds(r0, rc), cols]
            r_gate = _sigmoid(gates_s[pl.ds(r0, rc), cols] + gab_ref[:, cols])
            i_gate = _sigmoid(gates_s[pl.ds(r0, rc), pl.ds(d_lru + cols.start, SLAB)]
                              + gxb_ref[:, cols])
            log_a = neg_c_softplus * r_gate
            a = jnp.exp(log_a)
            b = jnp.sqrt(-jnp.tanh(log_a) * (a * a + 1.0)) * (i_gate * cv)
            a3 = a.reshape(groups, SUBLANES, SLAB)
            b3 = b.reshape(groups, SUBLANES, SLAB)
            for s in (1, 2, 4):
                keep = row_in_group >= s
                a_prev = jnp.where(keep, pltpu.roll(a3, s, axis=1), 1.0)
                b_prev = jnp.where(keep, pltpu.roll(b3, s, axis=1), 0.0)
                b3 = a3 * b_prev + b3
                a3 = a3 * a_prev
            hs = []
            for gi in range(groups):
                hg = b3[gi] + a3[gi] * hp
                hs.append(hg)
                hp = jnp.broadcast_to(hg[SUBLANES - 1:SUBLANES, :], (SUBLANES, SLAB))
            hnext.append(hp)
            gl = proj_s[pl.ds(r0 + SUBLANES, rc), pl.ds(d_lru + cols.start, SLAB)]
            ys.append(jnp.concatenate(hs, axis=0) * _gelu_tanh(gl))
        scale = lax.rsqrt(sum(jnp.sum(y * y, axis=-1, keepdims=True) for y in ys)
                          * (1.0 / d_lru) + EPS)
        for cols, y in zip(slab_cols(d_lru), ys):
            y_s[pl.ds(r0, rc), cols] = (y * scale * onl_ref[:, cols]).astype(jnp.bfloat16)
        return hnext

    def conv31(r0, after=None):
        ds = []
        for cols in slab_cols(d_cm):
            acc = jnp.broadcast_to(dwb_ref[:, cols], (rc, SLAB)).reshape(groups, SUBLANES, SLAB)
            for k in range(CONV_LONG):
                q, r = divmod(first_tap + k, SUBLANES)
                src = ext_s if r == 0 else sh_s.at[r - 1]
                tap = src[pl.ds(r0 + q * SUBLANES, rc), cols].reshape(groups, SUBLANES, SLAB)
                wk = _tie(dww_ref[k * SUBLANES:(k + 1) * SUBLANES, cols], after)
                acc = acc + wk[None] * tap
            ds.append(acc.reshape(rc, SLAB))
        mu = sum(jnp.sum(d, axis=-1, keepdims=True) for d in ds) * (1.0 / d_cm)
        dcs = [d - mu for d in ds]
        var = sum(jnp.sum(dc * dc, axis=-1, keepdims=True) for dc in dcs) * (1.0 / d_cm)
        inv = lax.rsqrt(var + EPS)
        ys = []
        for cols, dc in zip(slab_cols(d_cm), dcs):
            ln = dc * inv * lng_ref[:, cols] + lnb_ref[:, cols]
            ys.append(ln * _sigmoid(ln))
        scale = lax.rsqrt(sum(jnp.sum(y * y, axis=-1, keepdims=True) for y in ys)
                          * (1.0 / d_cm) + EPS)
        for cols, y in zip(slab_cols(d_cm), ys):
            y_s[pl.ds(r0, rc), pl.ds(d_lru + cols.start, SLAB)] = (
                y * scale * onc_ref[:, cols]).astype(jnp.bfloat16)

    def out_proj():
        x1_ref[...] = x_ref[...] + jnp.dot(y_s[...], w_out_ref[...],
                                           preferred_element_type=jnp.float32)
        proj_s[0:SUBLANES, col_xl] = proj_s[ts:ts + SUBLANES, col_xl]
        ext_s[0:HALO_LONG, 0:d_cm] = ext_s[ts:ts + HALO_LONG, 0:d_cm]

    return dict(norm=norm, in_proj=in_proj, conv4_glu=conv4_glu, shift=shift, gates=gates,
                lru=lru, conv31=conv31, out_proj=out_proj,
                col_xl=col_xl, col_gl=col_gl, col_vg=col_vg)


def _ffn_pieces(x_ref, o_ref, nffn_ref, wg_ref, wu_ref, wd_ref, nfin_ref, hf_s, f_s, yf_s,
                *, final_norm):
    ts, d_model = x_ref.shape

    def norm():
        xv = x_ref[...]
        hf_s[...] = (xv * _rms_scale(xv, d_model) * nffn_ref[...]).astype(jnp.bfloat16)

    def gate_up(c, part):
        rows = slice(part * FFN_SUB_ROWS, (part + 1) * FFN_SUB_ROWS)
        cols = slice(c * FFN_COL_CHUNK, (c + 1) * FFN_COL_CHUNK)
        h = hf_s[rows, :]
        g = jnp.dot(h, wg_ref[:, cols], preferred_element_type=jnp.float32)
        u = jnp.dot(h, wu_ref[:, cols], preferred_element_type=jnp.float32)
        f = _silu(g) * u
        f_s[rows, cols] = f.astype(jnp.bfloat16)
        return f[-SUBLANES:, -LANES:]

    def down(n, part):
        rows = slice(part * FFN_SUB_ROWS, (part + 1) * FFN_SUB_ROWS)
        cols = slice(n * FFN_COL_CHUNK, (n + 1) * FFN_COL_CHUNK)
        y = x_ref[rows, cols] + jnp.dot(f_s[rows, :], wd_ref[:, cols],
                                        preferred_element_type=jnp.float32)
        yf_s[rows, cols] = y
        return y[-SUBLANES:, -LANES:]

    def finish():
        y = yf_s[...]
        if final_norm:
            y = y * _rms_scale(y, d_model) * nfin_ref[...]
        o_ref[...] = y

    return dict(norm=norm, gate_up=gate_up, down=down, finish=finish)


def _block_kernel(x_ref, nmix_ref, w_in_ref, c4w_ref, c4b_ref, wgate_ref, gab_ref, gxb_ref,
                  lam_ref, dww_ref, dwb_ref, lng_ref, lnb_ref, onl_ref, onc_ref, w_out_ref,
                  nffn_ref, wg_ref, wu_ref, wd_ref, nfin_ref,
                  o_ref,
                  x1_s, hf_s, f_s, yf_s, h_s, proj_s, c_s, cb_s, gates_s, ext_s, sh_s, y_s, hc_s,
                  *, tiles_per_seq, final_norm):
    t = pl.program_id(0)
    ts, d_model = x_ref.shape
    d_lru = c4b_ref.shape[1]
    d_cm = dwb_ref.shape[1]
    d_ff = wd_ref.shape[0]
    rc = ROW_CHUNK
    slot = lax.rem(t, 2)

    @pl.when(t == 0)
    def _():
        x1_s[1] = jnp.zeros(x1_s.shape[1:], jnp.float32)

    @pl.when(lax.rem(t, tiles_per_seq) == 0)
    def _():
        proj_s[0:SUBLANES, 0:d_lru] = jnp.zeros((SUBLANES, d_lru), jnp.float32)
        ext_s[0:HALO_LONG, 0:d_cm] = jnp.zeros((HALO_LONG, d_cm), jnp.float32)
        hc_s[...] = jnp.zeros_like(hc_s)

    mix = _mixer_pieces(x_ref, x1_s.at[slot], nmix_ref, w_in_ref, c4w_ref, c4b_ref, wgate_ref,
                        gab_ref, gxb_ref, lam_ref, dww_ref, dwb_ref, lng_ref, lnb_ref, onl_ref,
                        onc_ref, w_out_ref, h_s, proj_s, c_s, cb_s, gates_s, ext_s, sh_s, y_s)
    ffn = _ffn_pieces(x1_s.at[1 - slot], o_ref, nffn_ref, wg_ref, wu_ref, wd_ref, nfin_ref,
                      hf_s, f_s, yf_s, final_norm=final_norm)

    row_chunks = list(range(0, ts, rc))
    n_gate_up = d_ff // FFN_COL_CHUNK
    n_down = d_model // FFN_COL_CHUNK
    pieces = []
    for part in range(ts // FFN_SUB_ROWS):
        pieces += [functools.partial(ffn["gate_up"], c, part) for c in range(n_gate_up)]
        pieces += [functools.partial(ffn["down"], n, part) for n in range(n_down)]
    ffn_matmuls = iter(pieces)

    def ffn_matmul(wanted=True):
        piece = next(ffn_matmuls, None) if wanted else None
        return None if piece is None else piece()

    ffn["norm"]()
    mix["norm"]()
    mix["in_proj"](mix["col_xl"])
    ffn_matmul()
    mix["in_proj"](mix["col_vg"])
    ffn_matmul()
    mix["in_proj"](mix["col_gl"])
    for i, r0 in enumerate(row_chunks):
        mix["conv4_glu"](r0)
        if i > 0:
            mix["shift"](row_chunks[i - 1])
    mix["shift"](row_chunks[-1])
    mix["shift"](ts, HALO_LONG - SUBLANES)
    for _ in range(EARLY_FFN_PIECES - 2):
        ffn_matmul()
    mix["gates"]()
    hprev = [hc_s[:, ls:ls + SLAB] for ls in range(0, d_lru, SLAB)]
    for i, r0 in enumerate(row_chunks):
        mix["conv31"](r0, after=ffn_matmul(i % 4 == 0))
        hprev = mix["lru"](r0, hprev, after=ffn_matmul())
    hc_s[...] = jnp.concatenate(hprev, axis=1)
    while ffn_matmul() is not None:
        pass
    mix["out_proj"]()
    ffn["finish"]()


def _resident(shape):
    zeros = (0,) * len(shape)
    return pl.BlockSpec(shape, lambda *_: zeros, pipeline_mode=pl.Buffered(1))


def _block_diag(w, block):
    heads, d, _ = w.shape
    per_block = block // d
    eye = jnp.eye(per_block, dtype=w.dtype)
    w = w.reshape(heads // per_block, per_block, d, d)
    return jnp.einsum('nhij,hg->nhigj', w, eye).reshape(heads // per_block, block, block)


def _layer(x, norm_mix, w_in, conv4_w, conv4_b, gate_a_w, gate_a_b, gate_x_w, gate_x_b,
           lru_lambda, dw31_w, dw31_b, cm_ln_g, cm_ln_b, out_norm_lru, out_norm_cm, w_out,
           norm_ffn, w_gate, w_up, w_down, norm_final, final_norm):
    bsz, seq, d_model = x.shape
    d_lru = conv4_b.shape[0]
    d_cm = dw31_b.shape[0]
    d_ff = w_gate.shape[1]
    ts = TOK_TILE
    n_tok = bsz * seq
    n_tiles = n_tok // ts
    assert seq % ts == 0 and ts % ROW_CHUNK == 0 and ts % FFN_SUB_ROWS == 0
    assert d_ff % FFN_COL_CHUNK == 0 and d_model % FFN_COL_CHUNK == 0
    assert d_lru % SLAB == 0 and d_cm % SLAB == 0
    assert conv4_w.shape[0] == CONV_SHORT and dw31_w.shape[0] == CONV_LONG
    bf16 = jnp.bfloat16
    f32 = jnp.float32
    row = lambda v: v.reshape(1, -1)
    assert MXU_DIM % gate_a_w.shape[1] == 0 and d_lru % MXU_DIM == 0
    w_gates = jnp.concatenate([_block_diag(gate_a_w, MXU_DIM), _block_diag(gate_x_w, MXU_DIM)],
                              axis=2).astype(bf16)
    dww_rows = jnp.repeat(dw31_w, SUBLANES, axis=0)
    operands = (
        x.reshape(n_tok, d_model), row(norm_mix), w_in.astype(bf16), conv4_w, row(conv4_b),
        w_gates, row(gate_a_b), row(gate_x_b), row(lru_lambda), dww_rows, row(dw31_b),
        row(cm_ln_g), row(cm_ln_b), row(out_norm_lru), row(out_norm_cm), w_out.astype(bf16),
        row(norm_ffn), w_gate.astype(bf16), w_up.astype(bf16), w_down.astype(bf16),
        row(norm_final))
    in_specs = [pl.BlockSpec((ts, d_model), lambda t: (jnp.minimum(t, n_tiles - 1), 0))]
    in_specs += [_resident(op.shape) for op in operands[1:]]
    scratch = [
        pltpu.VMEM((2, ts, d_model), f32),
        pltpu.VMEM((ts, d_model), bf16),
        pltpu.VMEM((ts, d_ff), bf16),
        pltpu.VMEM((ts, d_model), f32),
        pltpu.VMEM((ts, d_model), bf16),
        pltpu.VMEM((ts + SUBLANES, w_in.shape[1]), f32),
        pltpu.VMEM((ts, d_lru), f32),
        pltpu.VMEM((ts, d_lru), bf16),
        pltpu.VMEM((ts, 2 * d_lru), f32),
        pltpu.VMEM((ts + HALO_LONG, d_cm + LANES), f32),
        pltpu.VMEM((SUBLANES - 1, ts + HALO_LONG - SUBLANES, d_cm + LANES), f32),
        pltpu.VMEM((ts, d_lru + d_cm), bf16),
        pltpu.VMEM((SUBLANES, d_lru), f32),
    ]
    out = pl.pallas_call(
        functools.partial(_block_kernel, tiles_per_seq=seq // ts, final_norm=final_norm),
        grid=(n_tiles + 1,),
        in_specs=in_specs,
        out_specs=pl.BlockSpec((ts, d_model), lambda t: (jnp.maximum(t - 1, 0), 0)),
        out_shape=jax.ShapeDtypeStruct((n_tok, d_model), x.dtype),
        scratch_shapes=scratch,
        compiler_params=pltpu.CompilerParams(
            dimension_semantics=("arbitrary",),
            vmem_limit_bytes=VMEM_LIMIT_BYTES),
        name="block",
    )(*operands)
    return out.reshape(bsz, seq, d_model)


def kernel(x, norm_mix, w_in, conv4_w, conv4_b, gate_a_w, gate_a_b, gate_x_w, gate_x_b, lru_lambda, dw31_w, dw31_b, cm_ln_g, cm_ln_b, out_norm_lru, out_norm_cm, w_out, norm_ffn, w_gate, w_up, w_down, norm_final):
    depth = w_in.shape[0]
    for l in range(depth):
        x = _layer(x, norm_mix[l], w_in[l], conv4_w[l], conv4_b[l], gate_a_w[l], gate_a_b[l],
                   gate_x_w[l], gate_x_b[l], lru_lambda[l], dw31_w[l], dw31_b[l], cm_ln_g[l],
                   cm_ln_b[l], out_norm_lru[l], out_norm_cm[l], w_out[l], norm_ffn[l], w_gate[l],
                   w_up[l], w_down[l], norm_final, final_norm=(l == depth - 1))
    return x
```

```python
import functools
import math

import jax
import jax.numpy as jnp
from jax import lax
from jax.experimental import pallas as pl
from jax.experimental.pallas import tpu as pltpu

EPS = 1e-6
LRU_C = 8.0
SUBLANES = 8
LANES = 128
SLAB = 512
MXU_DIM = 256
CONV_SHORT = 4
CONV_LONG = 31
HALO_LONG = 32
EXT_SKEW_ROWS = 8
TOK_TILE = 512
FFN_SUB_ROWS = 256
ROW_CHUNK = 16
FFN_COL_CHUNK = 256
EARLY_FFN_PIECES = 4
VMEM_LIMIT_BYTES = 56 * 1024 * 1024


def _sigmoid(v):
    return 0.5 + 0.5 * jnp.tanh(0.5 * v)


def _silu(v):
    hv = 0.5 * v
    return hv + hv * jnp.tanh(hv)


def _gelu_tanh(v):
    inner = math.sqrt(2.0 / math.pi) * (v + 0.044715 * (v * v * v))
    return v * (0.5 * (1.0 + jnp.tanh(inner)))


def _rms_scale(v, width):
    return lax.rsqrt(jnp.sum(v * v, axis=-1, keepdims=True) * (1.0 / width) + EPS)


def _tie_mask(src, width):
    if src is None:
        return None
    src = jnp.tile(src, (1, width // LANES))
    return jnp.abs(src) < -1.0, src


def _tie(dst, tie):
    if tie is None:
        return dst
    never, src = tie
    dst3 = dst.reshape(-1, SUBLANES, dst.shape[-1])
    return jnp.where(never[None], src[None], dst3).reshape(dst.shape)


def _mixer_pieces(x_ref, x1_ref, nmix_ref, w_in_ref, c4w_ref, c4b_ref, wgate_ref, gab_ref, gxb_ref,
                  lam_ref, dww_ref, dwb_ref, lng_ref, lnb_ref, onl_ref, onc_ref, w_out_ref,
                  h_s, xl_s, proj_s, c_s, cb_s, gates_s, ext_s, y_s):
    ts, d_model = x_ref.shape
    d_lru = c4b_ref.shape[1]
    d_cm = dwb_ref.shape[1]
    rc = ROW_CHUNK
    groups = rc // SUBLANES
    first_tap = HALO_LONG - (CONV_LONG - 1)
    col_xl = slice(0, d_lru)
    col_gl = slice(d_lru, 2 * d_lru)
    col_vg = slice(2 * d_lru, 2 * d_lru + 2 * d_cm)

    def norm():
        nmix = nmix_ref[...]
        for r0 in range(0, ts, rc):
            xv = x_ref[pl.ds(r0, rc), :]
            h_s[pl.ds(r0, rc), :] = (xv * _rms_scale(xv, d_model) * nmix).astype(jnp.bfloat16)

    def in_proj(cols):
        res = jnp.dot(h_s[...], w_in_ref[:, cols], preferred_element_type=jnp.float32)
        if cols == col_xl:
            for ls in range(0, d_lru, LANES):
                xl_s[ls // LANES, pl.ds(SUBLANES, ts), :] = res[:, ls:ls + LANES]
        else:
            proj_s[:, cols] = res

    def slab_cols(width):
        return [slice(ls, ls + SLAB) for ls in range(0, width, SLAB)]

    def conv4_glu(r0):
        for cols in slab_cols(d_lru):
            c4w = c4w_ref[:, cols]
            acc = c4b_ref[:, cols]
            for s in range(CONV_SHORT):
                tap = jnp.concatenate(
                    [xl_s[(cols.start + ls) // LANES, pl.ds(r0 + SUBLANES - s, rc), :]
                     for ls in range(0, SLAB, LANES)], axis=1)
                acc = acc + c4w[CONV_SHORT - 1 - s:CONV_SHORT - s, :] * tap
            c_s[pl.ds(r0, rc), cols] = acc
            cb_s[pl.ds(r0, rc), cols] = acc.astype(jnp.bfloat16)
        for cols in slab_cols(d_cm):
            v = proj_s[pl.ds(r0, rc), pl.ds(2 * d_lru + cols.start, SLAB)]
            g = proj_s[pl.ds(r0, rc), pl.ds(2 * d_lru + d_cm + cols.start, SLAB)]
            glu = v * _sigmoid(g)
            for ls in range(0, SLAB, LANES):
                ext_s[(cols.start + ls) // LANES, pl.ds(r0 + HALO_LONG, rc), :] = (
                    glu[:, ls:ls + LANES])

    def gates():
        blk = wgate_ref.shape[1]
        for b in range(wgate_ref.shape[0]):
            res = jnp.dot(cb_s[:, b * blk:(b + 1) * blk], wgate_ref[b],
                          preferred_element_type=jnp.float32)
            gates_s[:, b * blk:(b + 1) * blk] = res[:, :blk]
            gates_s[:, d_lru + b * blk:d_lru + (b + 1) * blk] = res[:, blk:]

    def lru(r0, hprev, after=None):
        row_in_group = lax.broadcasted_iota(jnp.int32, (groups, SUBLANES, SLAB), 1)
        tie = _tie_mask(after, SLAB)
        hprev = [_tie(hp, tie) for hp in hprev]
        ys, hnext = [], []
        for cols, hp in zip(slab_cols(d_lru), hprev):
            z = -lam_ref[:, cols]
            neg_c_softplus = -LRU_C * (jnp.maximum(z, 0.0) + jnp.log1p(jnp.exp(-jnp.abs(z))))
            cv = c_s[pl.ds(r0, rc), cols]
            r_gate = _sigmoid(gates_s[pl.ds(r0, rc), cols] + gab_ref[:, cols])
            i_gate = _sigmoid(gates_s[pl.ds(r0, rc), pl.ds(d_lru + cols.start, SLAB)]
                              + gxb_ref[:, cols])
            log_a = neg_c_softplus * r_gate
            a = jnp.exp(log_a)
            b = jnp.sqrt(-jnp.tanh(log_a) * (a * a + 1.0)) * (i_gate * cv)
            a3 = a.reshape(groups, SUBLANES, SLAB)
            b3 = b.reshape(groups, SUBLANES, SLAB)
            for s in (1, 2, 4):
                keep = row_in_group >= s
                a_prev = jnp.where(keep, pltpu.roll(a3, s, axis=1), 1.0)
                b_prev = jnp.where(keep, pltpu.roll(b3, s, axis=1), 0.0)
                b3 = a3 * b_prev + b3
                a3 = a3 * a_prev
            hs = []
            for gi in range(groups):
                hg = b3[gi] + a3[gi] * hp
                hs.append(hg)
                hp = jnp.broadcast_to(hg[SUBLANES - 1:SUBLANES, :], (SUBLANES, SLAB))
            hnext.append(hp)
            gl = proj_s[pl.ds(r0, rc), pl.ds(d_lru + cols.start, SLAB)]
            ys.append(jnp.concatenate(hs, axis=0) * _gelu_tanh(gl))
        scale = lax.rsqrt(sum(jnp.sum(y * y, axis=-1, keepdims=True) for y in ys)
                          * (1.0 / d_lru) + EPS)
        for cols, y in zip(slab_cols(d_lru), ys):
            y_s[pl.ds(r0, rc), cols] = (y * scale * onl_ref[:, cols]).astype(jnp.bfloat16)
        return hnext

    def conv31(r0, after=None):
        ds = []
        tie = _tie_mask(after, SLAB)
        for cols in slab_cols(d_cm):
            acc = jnp.broadcast_to(dwb_ref[:, cols], (rc, SLAB)).reshape(groups, SUBLANES, SLAB)
            for k in range(CONV_LONG):
                tap = jnp.concatenate(
                    [ext_s[(cols.start + ls) // LANES, pl.ds(r0 + first_tap + k, rc), :]
                     for ls in range(0, SLAB, LANES)], axis=1).reshape(groups, SUBLANES, SLAB)
                wk = _tie(dww_ref[k * SUBLANES:(k + 1) * SUBLANES, cols], tie)
                acc = acc + wk[None] * tap
            ds.append(acc.reshape(rc, SLAB))
        mu = sum(jnp.sum(d, axis=-1, keepdims=True) for d in ds) * (1.0 / d_cm)
        dcs = [d - mu for d in ds]
        var = sum(jnp.sum(dc * dc, axis=-1, keepdims=True) for dc in dcs) * (1.0 / d_cm)
        inv = lax.rsqrt(var + EPS)
        ys = []
        for cols, dc in zip(slab_cols(d_cm), dcs):
            ln = dc * inv * lng_ref[:, cols] + lnb_ref[:, cols]
            ys.append(ln * _sigmoid(ln))
        scale = lax.rsqrt(sum(jnp.sum(y * y, axis=-1, keepdims=True) for y in ys)
                          * (1.0 / d_cm) + EPS)
        for cols, y in zip(slab_cols(d_cm), ys):
            y_s[pl.ds(r0, rc), pl.ds(d_lru + cols.start, SLAB)] = (
                y * scale * onc_ref[:, cols]).astype(jnp.bfloat16)

    def out_proj():
        x1_ref[...] = x_ref[...] + jnp.dot(y_s[...], w_out_ref[...],
                                           preferred_element_type=jnp.float32)
        xl_s[:, 0:SUBLANES, :] = xl_s[:, ts:ts + SUBLANES, :]
        ext_s[:, 0:HALO_LONG, :] = ext_s[:, ts:ts + HALO_LONG, :]

    return dict(norm=norm, in_proj=in_proj, conv4_glu=conv4_glu, gates=gates,
                lru=lru, conv31=conv31, out_proj=out_proj,
                col_xl=col_xl, col_gl=col_gl, col_vg=col_vg)


def _ffn_pieces(x_ref, o_ref, nffn_ref, wg_ref, wu_ref, wd_ref, nfin_ref, hf_s, f_s, yf_s,
                *, final_norm):
    ts, d_model = x_ref.shape

    def norm():
        xv = x_ref[...]
        hf_s[...] = (xv * _rms_scale(xv, d_model) * nffn_ref[...]).astype(jnp.bfloat16)

    def gate_up(c, part):
        rows = slice(part * FFN_SUB_ROWS, (part + 1) * FFN_SUB_ROWS)
        cols = slice(c * FFN_COL_CHUNK, (c + 1) * FFN_COL_CHUNK)
        h = hf_s[rows, :]
        g = jnp.dot(h, wg_ref[:, cols], preferred_element_type=jnp.float32)
        u = jnp.dot(h, wu_ref[:, cols], preferred_element_type=jnp.float32)
        f = _silu(g) * u
        f_s[rows, cols] = f.astype(jnp.bfloat16)
        return f[-SUBLANES:, -LANES:]

    def down(n, part):
        rows = slice(part * FFN_SUB_ROWS, (part + 1) * FFN_SUB_ROWS)
        cols = slice(n * FFN_COL_CHUNK, (n + 1) * FFN_COL_CHUNK)
        y = x_ref[rows, cols] + jnp.dot(f_s[rows, :], wd_ref[:, cols],
                                        preferred_element_type=jnp.float32)
        yf_s[rows, cols] = y
        return y[-SUBLANES:, -LANES:]

    def finish():
        y = yf_s[...]
        if final_norm:
            y = y * _rms_scale(y, d_model) * nfin_ref[...]
        o_ref[...] = y

    return dict(norm=norm, gate_up=gate_up, down=down, finish=finish)


def _block_kernel(x_ref, nmix_ref, w_in_ref, c4w_ref, c4b_ref, wgate_ref, gab_ref, gxb_ref,
                  lam_ref, dww_ref, dwb_ref, lng_ref, lnb_ref, onl_ref, onc_ref, w_out_ref,
                  nffn_ref, wg_ref, wu_ref, wd_ref, nfin_ref,
                  o_ref,
                  x1_s, hf_s, f_s, yf_s, h_s, xl_s, proj_s, c_s, cb_s, gates_s, ext_s, y_s, hc_s,
                  *, tiles_per_seq, final_norm):
    t = pl.program_id(0)
    ts, d_model = x_ref.shape
    d_lru = c4b_ref.shape[1]
    d_cm = dwb_ref.shape[1]
    d_ff = wd_ref.shape[0]
    rc = ROW_CHUNK
    slot = lax.rem(t, 2)

    @pl.when(t == 0)
    def _():
        x1_s[1] = jnp.zeros(x1_s.shape[1:], jnp.float32)

    @pl.when(lax.rem(t, tiles_per_seq) == 0)
    def _():
        xl_s[:, 0:SUBLANES, :] = jnp.zeros((xl_s.shape[0], SUBLANES, LANES), jnp.float32)
        ext_s[:, 0:HALO_LONG, :] = jnp.zeros((ext_s.shape[0], HALO_LONG, LANES), jnp.float32)
        hc_s[...] = jnp.zeros_like(hc_s)

    mix = _mixer_pieces(x_ref, x1_s.at[slot], nmix_ref, w_in_ref, c4w_ref, c4b_ref, wgate_ref,
                        gab_ref, gxb_ref, lam_ref, dww_ref, dwb_ref, lng_ref, lnb_ref, onl_ref,
                        onc_ref, w_out_ref, h_s, xl_s, proj_s, c_s, cb_s, gates_s, ext_s, y_s)
    ffn = _ffn_pieces(x1_s.at[1 - slot], o_ref, nffn_ref, wg_ref, wu_ref, wd_ref, nfin_ref,
                      hf_s, f_s, yf_s, final_norm=final_norm)

    row_chunks = list(range(0, ts, rc))
    n_gate_up = d_ff // FFN_COL_CHUNK
    n_down = d_model // FFN_COL_CHUNK
    pieces = []
    for part in range(ts // FFN_SUB_ROWS):
        pieces += [functools.partial(ffn["gate_up"], c, part) for c in range(n_gate_up)]
        pieces += [functools.partial(ffn["down"], n, part) for n in range(n_down)]
    ffn_matmuls = iter(pieces)

    def ffn_matmul(wanted=True):
        piece = next(ffn_matmuls, None) if wanted else None
        return None if piece is None else piece()

    ffn["norm"]()
    mix["norm"]()
    mix["in_proj"](mix["col_xl"])
    ffn_matmul()
    mix["in_proj"](mix["col_vg"])
    ffn_matmul()
    mix["in_proj"](mix["col_gl"])
    for i, r0 in enumerate(row_chunks):
        mix["conv4_glu"](r0)
    for _ in range(EARLY_FFN_PIECES - 2):
        ffn_matmul()
    mix["gates"]()
    hprev = [hc_s[:, ls:ls + SLAB] for ls in range(0, d_lru, SLAB)]
    for i, r0 in enumerate(row_chunks):
        mix["conv31"](r0, after=ffn_matmul(i % 4 == 0))
        hprev = mix["lru"](r0, hprev, after=ffn_matmul())
    hc_s[...] = jnp.concatenate(hprev, axis=1)
    while ffn_matmul() is not None:
        pass
    mix["out_proj"]()
    ffn["finish"]()


def _resident(shape):
    zeros = (0,) * len(shape)
    return pl.BlockSpec(shape, lambda *_: zeros, pipeline_mode=pl.Buffered(1))


def _block_diag(w, block):
    heads, d, _ = w.shape
    per_block = block // d
    eye = jnp.eye(per_block, dtype=w.dtype)
    w = w.reshape(heads // per_block, per_block, d, d)
    return jnp.einsum('nhij,hg->nhigj', w, eye).reshape(heads // per_block, block, block)


def _layer(x, norm_mix, w_in, conv4_w, conv4_b, gate_a_w, gate_a_b, gate_x_w, gate_x_b,
           lru_lambda, dw31_w, dw31_b, cm_ln_g, cm_ln_b, out_norm_lru, out_norm_cm, w_out,
           norm_ffn, w_gate, w_up, w_down, norm_final, final_norm):
    bsz, seq, d_model = x.shape
    d_lru = conv4_b.shape[0]
    d_cm = dw31_b.shape[0]
    d_ff = w_gate.shape[1]
    ts = TOK_TILE
    n_tok = bsz * seq
    n_tiles = n_tok // ts
    assert seq % ts == 0 and ts % ROW_CHUNK == 0 and ts % FFN_SUB_ROWS == 0
    assert d_ff % FFN_COL_CHUNK == 0 and d_model % FFN_COL_CHUNK == 0
    assert d_lru % SLAB == 0 and d_cm % SLAB == 0
    assert conv4_w.shape[0] == CONV_SHORT and dw31_w.shape[0] == CONV_LONG
    bf16 = jnp.bfloat16
    f32 = jnp.float32
    row = lambda v: v.reshape(1, -1)
    assert MXU_DIM % gate_a_w.shape[1] == 0 and d_lru % MXU_DIM == 0
    w_gates = jnp.concatenate([_block_diag(gate_a_w, MXU_DIM), _block_diag(gate_x_w, MXU_DIM)],
                              axis=2).astype(bf16)
    dww_rows = jnp.repeat(dw31_w, SUBLANES, axis=0)
    operands = (
        x.reshape(n_tok, d_model), row(norm_mix), w_in.astype(bf16), conv4_w, row(conv4_b),
        w_gates, row(gate_a_b), row(gate_x_b), row(lru_lambda), dww_rows, row(dw31_b),
        row(cm_ln_g), row(cm_ln_b), row(out_norm_lru), row(out_norm_cm), w_out.astype(bf16),
        row(norm_ffn), w_gate.astype(bf16), w_up.astype(bf16), w_down.astype(bf16),
        row(norm_final))
    in_specs = [pl.BlockSpec((ts, d_model), lambda t: (jnp.minimum(t, n_tiles - 1), 0))]
    in_specs += [_resident(op.shape) for op in operands[1:]]
    scratch = [
        pltpu.VMEM((2, ts, d_model), f32),
        pltpu.VMEM((ts, d_model), bf16),
        pltpu.VMEM((ts, d_ff), bf16),
        pltpu.VMEM((ts, d_model), f32),
        pltpu.VMEM((ts, d_model), bf16),
        pltpu.VMEM((d_lru // LANES, ts + 2 * SUBLANES, LANES), f32),
        pltpu.VMEM((ts, w_in.shape[1]), f32),
        pltpu.VMEM((ts, d_lru), f32),
        pltpu.VMEM((ts, d_lru), bf16),
        pltpu.VMEM((ts, 2 * d_lru), f32),
        pltpu.VMEM((d_cm // LANES, ts + HALO_LONG + EXT_SKEW_ROWS, LANES), f32),
        pltpu.VMEM((ts, d_lru + d_cm), bf16),
        pltpu.VMEM((SUBLANES, d_lru), f32),
    ]
    out = pl.pallas_call(
        functools.partial(_block_kernel, tiles_per_seq=seq // ts, final_norm=final_norm),
        grid=(n_tiles + 1,),
        in_specs=in_specs,
        out_specs=pl.BlockSpec((ts, d_model), lambda t: (jnp.maximum(t - 1, 0), 0)),
        out_shape=jax.ShapeDtypeStruct((n_tok, d_model), x.dtype),
        scratch_shapes=scratch,
        compiler_params=pltpu.CompilerParams(
            dimension_semantics=("arbitrary",),
            vmem_limit_bytes=VMEM_LIMIT_BYTES),
        name="block",
    )(*operands)
    return out.reshape(bsz, seq, d_model)


def kernel(x, norm_mix, w_in, conv4_w, conv4_b, gate_a_w, gate_a_b, gate_x_w, gate_x_b, lru_lambda, dw31_w, dw31_b, cm_ln_g, cm_ln_b, out_norm_lru, out_norm_cm, w_out, norm_ffn, w_gate, w_up, w_down, norm_final):
    depth = w_in.shape[0]
    for l in range(depth):
        x = _layer(x, norm_mix[l], w_in[l], conv4_w[l], conv4_b[l], gate_a_w[l], gate_a_b[l],
                   gate_x_w[l], gate_x_b[l], lru_lambda[l], dw31_w[l], dw31_b[l], cm_ln_g[l],
                   cm_ln_b[l], out_norm_lru[l], out_norm_cm[l], w_out[l], norm_ffn[l], w_gate[l],
                   w_up[l], w_down[l], norm_final, final_norm=(l == depth - 1))
    return x
```

```python
import functools
import math

import jax
import jax.numpy as jnp
from jax import lax
from jax.experimental import pallas as pl
from jax.experimental.pallas import tpu as pltpu

EPS = 1e-6
LRU_C = 8.0
SUBLANES = 8
LANES = 128
SLAB = 512
MXU_DIM = 256
CONV_SHORT = 4
CONV_LONG = 31
HALO_LONG = 32
EXT_SKEW_ROWS = 8
TOK_TILE = 512
FFN_SUB_ROWS = 256
ROW_CHUNK = 16
FFN_COL_CHUNK = 256
TIE_EVERY = 3
CONV_LEAD = 8
VMEM_LIMIT_BYTES = 56 * 1024 * 1024


def _sigmoid(v):
    return 0.5 + 0.5 * jnp.tanh(0.5 * v)


def _silu(v):
    hv = 0.5 * v
    return hv + hv * jnp.tanh(hv)


def _gelu_tanh(v):
    inner = math.sqrt(2.0 / math.pi) * (v + 0.044715 * (v * v * v))
    return v * (0.5 * (1.0 + jnp.tanh(inner)))


def _rms_scale(v, width):
    return lax.rsqrt(jnp.sum(v * v, axis=-1, keepdims=True) * (1.0 / width) + EPS)


def _tie_mask(src, width):
    if src is None:
        return None
    src = jnp.tile(src, (1, width // LANES))
    return jnp.abs(src) < -1.0, src


def _tie(dst, tie):
    if tie is None:
        return dst
    never, src = tie
    dst3 = dst.reshape(-1, SUBLANES, dst.shape[-1])
    return jnp.where(never[None], src[None], dst3).reshape(dst.shape)


def _mixer_pieces(x_ref, x1_ref, nmix_ref, w_in_ref, c4w_ref, c4b_ref, wgate_ref, gab_ref, gxb_ref,
                  lam_ref, dww_ref, dwb_ref, lng_ref, lnb_ref, onl_ref, onc_ref, w_out_ref,
                  h_s, xl_s, proj_s, c_s, cb_s, gates_s, ext_s, y_s):
    ts, d_model = x_ref.shape
    d_lru = c4b_ref.shape[1]
    d_cm = dwb_ref.shape[1]
    rc = ROW_CHUNK
    groups = rc // SUBLANES
    first_tap = HALO_LONG - (CONV_LONG - 1)
    col_xl = slice(0, d_lru)
    col_gl = slice(d_lru, 2 * d_lru)
    col_vg = slice(2 * d_lru, 2 * d_lru + 2 * d_cm)

    def norm():
        nmix = nmix_ref[...]
        for r0 in range(0, ts, rc):
            xv = x_ref[pl.ds(r0, rc), :]
            h_s[pl.ds(r0, rc), :] = (xv * _rms_scale(xv, d_model) * nmix).astype(jnp.bfloat16)

    def in_proj(cols):
        res = jnp.dot(h_s[...], w_in_ref[:, cols], preferred_element_type=jnp.float32)
        if cols == col_xl:
            for ls in range(0, d_lru, LANES):
                xl_s[ls // LANES, pl.ds(SUBLANES, ts), :] = res[:, ls:ls + LANES]
        else:
            proj_s[:, cols] = res

    def slab_cols(width):
        return [slice(ls, ls + SLAB) for ls in range(0, width, SLAB)]

    def conv4_glu(r0, after=None):
        tie = _tie_mask(after, SLAB)
        for cols in slab_cols(d_lru):
            c4w = c4w_ref[:, cols]
            acc = _tie(jnp.broadcast_to(c4b_ref[:, cols], (SUBLANES, SLAB)), tie)[0:1, :]
            for s in range(CONV_SHORT):
                tap = jnp.concatenate(
                    [xl_s[(cols.start + ls) // LANES, pl.ds(r0 + SUBLANES - s, rc), :]
                     for ls in range(0, SLAB, LANES)], axis=1)
                acc = acc + c4w[CONV_SHORT - 1 - s:CONV_SHORT - s, :] * tap
            c_s[pl.ds(r0, rc), cols] = acc
            cb_s[pl.ds(r0, rc), cols] = acc.astype(jnp.bfloat16)
        for cols in slab_cols(d_cm):
            v = proj_s[pl.ds(r0, rc), pl.ds(2 * d_lru + cols.start, SLAB)]
            g = proj_s[pl.ds(r0, rc), pl.ds(2 * d_lru + d_cm + cols.start, SLAB)]
            glu = v * _sigmoid(g)
            for ls in range(0, SLAB, LANES):
                ext_s[(cols.start + ls) // LANES, pl.ds(r0 + HALO_LONG, rc), :] = (
                    glu[:, ls:ls + LANES])

    def gates():
        blk = wgate_ref.shape[1]
        for b in range(wgate_ref.shape[0]):
            res = jnp.dot(cb_s[:, b * blk:(b + 1) * blk], wgate_ref[b],
                          preferred_element_type=jnp.float32)
            gates_s[:, b * blk:(b + 1) * blk] = res[:, :blk]
            gates_s[:, d_lru + b * blk:d_lru + (b + 1) * blk] = res[:, blk:]

    def lru(r0, hprev, after=None):
        row_in_group = lax.broadcasted_iota(jnp.int32, (groups, SUBLANES, SLAB), 1)
        tie = _tie_mask(after, SLAB)
        hprev = [_tie(hp, tie) for hp in hprev]
        ys, hnext = [], []
        for cols, hp in zip(slab_cols(d_lru), hprev):
            z = -lam_ref[:, cols]
            neg_c_softplus = -LRU_C * (jnp.maximum(z, 0.0) + jnp.log1p(jnp.exp(-jnp.abs(z))))
            cv = c_s[pl.ds(r0, rc), cols]
            r_gate = _sigmoid(gates_s[pl.ds(r0, rc), cols] + gab_ref[:, cols])
            i_gate = _sigmoid(gates_s[pl.ds(r0, rc), pl.ds(d_lru + cols.start, SLAB)]
                              + gxb_ref[:, cols])
            log_a = neg_c_softplus * r_gate
            a = jnp.exp(log_a)
            b = jnp.sqrt(-jnp.tanh(log_a) * (a * a + 1.0)) * (i_gate * cv)
            a3 = a.reshape(groups, SUBLANES, SLAB)
            b3 = b.reshape(groups, SUBLANES, SLAB)
            for s in (1, 2, 4):
                keep = row_in_group >= s
                a_prev = jnp.where(keep, pltpu.roll(a3, s, axis=1), 1.0)
                b_prev = jnp.where(keep, pltpu.roll(b3, s, axis=1), 0.0)
                b3 = a3 * b_prev + b3
                a3 = a3 * a_prev
            hs = []
            for gi in range(groups):
                hg = b3[gi] + a3[gi] * hp
                hs.append(hg)
                hp = jnp.broadcast_to(hg[SUBLANES - 1:SUBLANES, :], (SUBLANES, SLAB))
            hnext.append(hp)
            gl = proj_s[pl.ds(r0, rc), pl.ds(d_lru + cols.start, SLAB)]
            ys.append(jnp.concatenate(hs, axis=0) * _gelu_tanh(gl))
        scale = lax.rsqrt(sum(jnp.sum(y * y, axis=-1, keepdims=True) for y in ys)
                          * (1.0 / d_lru) + EPS)
        for cols, y in zip(slab_cols(d_lru), ys):
            y_s[pl.ds(r0, rc), cols] = (y * scale * onl_ref[:, cols]).astype(jnp.bfloat16)
        return hnext

    def conv31(r0, after=None):
        ds = []
        tie = _tie_mask(after, SLAB)
        for cols in slab_cols(d_cm):
            acc = jnp.broadcast_to(dwb_ref[:, cols], (rc, SLAB)).reshape(groups, SUBLANES, SLAB)
            for k in range(CONV_LONG):
                tap = jnp.concatenate(
                    [ext_s[(cols.start + ls) // LANES, pl.ds(r0 + first_tap + k, rc), :]
                     for ls in range(0, SLAB, LANES)], axis=1).reshape(groups, SUBLANES, SLAB)
                wk = _tie(dww_ref[k * SUBLANES:(k + 1) * SUBLANES, cols], tie)
                acc = acc + wk[None] * tap
            ds.append(acc.reshape(rc, SLAB))
        mu = sum(jnp.sum(d, axis=-1, keepdims=True) for d in ds) * (1.0 / d_cm)
        dcs = [d - mu for d in ds]
        var = sum(jnp.sum(dc * dc, axis=-1, keepdims=True) for dc in dcs) * (1.0 / d_cm)
        inv = lax.rsqrt(var + EPS)
        ys = []
        for cols, dc in zip(slab_cols(d_cm), dcs):
            ln = dc * inv * lng_ref[:, cols] + lnb_ref[:, cols]
            ys.append(ln * _sigmoid(ln))
        scale = lax.rsqrt(sum(jnp.sum(y * y, axis=-1, keepdims=True) for y in ys)
                          * (1.0 / d_cm) + EPS)
        for cols, y in zip(slab_cols(d_cm), ys):
            y_s[pl.ds(r0, rc), pl.ds(d_lru + cols.start, SLAB)] = (
                y * scale * onc_ref[:, cols]).astype(jnp.bfloat16)
        return ys[0][0:SUBLANES, 0:LANES]

    def out_proj():
        x1_ref[...] = x_ref[...] + jnp.dot(y_s[...], w_out_ref[...],
                                           preferred_element_type=jnp.float32)
        xl_s[:, 0:SUBLANES, :] = xl_s[:, ts:ts + SUBLANES, :]
        ext_s[:, 0:HALO_LONG, :] = ext_s[:, ts:ts + HALO_LONG, :]

    return dict(norm=norm, in_proj=in_proj, conv4_glu=conv4_glu, gates=gates,
                lru=lru, conv31=conv31, out_proj=out_proj,
                col_xl=col_xl, col_gl=col_gl, col_vg=col_vg)


def _ffn_pieces(x_ref, o_ref, nffn_ref, wg_ref, wu_ref, wd_ref, nfin_ref, hf_s, f_s, yf_s,
                *, final_norm):
    ts, d_model = x_ref.shape

    def norm():
        xv = x_ref[...]
        hf_s[...] = (xv * _rms_scale(xv, d_model) * nffn_ref[...]).astype(jnp.bfloat16)

    def gate_up(c, part):
        rows = slice(part * FFN_SUB_ROWS, (part + 1) * FFN_SUB_ROWS)
        cols = slice(c * FFN_COL_CHUNK, (c + 1) * FFN_COL_CHUNK)
        h = hf_s[rows, :]
        g = jnp.dot(h, wg_ref[:, cols], preferred_element_type=jnp.float32)
        u = jnp.dot(h, wu_ref[:, cols], preferred_element_type=jnp.float32)
        f = _silu(g) * u
        f_s[rows, cols] = f.astype(jnp.bfloat16)
        return f[-SUBLANES:, -LANES:]

    def down(n, part):
        rows = slice(part * FFN_SUB_ROWS, (part + 1) * FFN_SUB_ROWS)
        cols = slice(n * FFN_COL_CHUNK, (n + 1) * FFN_COL_CHUNK)
        y = x_ref[rows, cols] + jnp.dot(f_s[rows, :], wd_ref[:, cols],
                                        preferred_element_type=jnp.float32)
        yf_s[rows, cols] = y
        return y[-SUBLANES:, -LANES:]

    def finish():
        y = yf_s[...]
        if final_norm:
            y = y * _rms_scale(y, d_model) * nfin_ref[...]
        o_ref[...] = y

    return dict(norm=norm, gate_up=gate_up, down=down, finish=finish)


def _block_kernel(x_ref, nmix_ref, w_in_ref, c4w_ref, c4b_ref, wgate_ref, gab_ref, gxb_ref,
                  lam_ref, dww_ref, dwb_ref, lng_ref, lnb_ref, onl_ref, onc_ref, w_out_ref,
                  nffn_ref, wg_ref, wu_ref, wd_ref, nfin_ref,
                  o_ref,
                  x1_s, hf_s, f_s, yf_s, h_s, xl_s, proj_s, c_s, cb_s, gates_s, ext_s, y_s, hc_s,
                  *, tiles_per_seq, final_norm):
    t = pl.program_id(0)
    ts, d_model = x_ref.shape
    d_lru = c4b_ref.shape[1]
    d_cm = dwb_ref.shape[1]
    d_ff = wd_ref.shape[0]
    rc = ROW_CHUNK
    slot = lax.rem(t, 2)

    @pl.when(t == 0)
    def _():
        x1_s[1] = jnp.zeros(x1_s.shape[1:], jnp.float32)

    @pl.when(lax.rem(t, tiles_per_seq) == 0)
    def _():
        xl_s[:, 0:SUBLANES, :] = jnp.zeros((xl_s.shape[0], SUBLANES, LANES), jnp.float32)
        ext_s[:, 0:HALO_LONG, :] = jnp.zeros((ext_s.shape[0], HALO_LONG, LANES), jnp.float32)
        hc_s[...] = jnp.zeros_like(hc_s)

    mix = _mixer_pieces(x_ref, x1_s.at[slot], nmix_ref, w_in_ref, c4w_ref, c4b_ref, wgate_ref,
                        gab_ref, gxb_ref, lam_ref, dww_ref, dwb_ref, lng_ref, lnb_ref, onl_ref,
                        onc_ref, w_out_ref, h_s, xl_s, proj_s, c_s, cb_s, gates_s, ext_s, y_s)
    ffn = _ffn_pieces(x1_s.at[1 - slot], o_ref, nffn_ref, wg_ref, wu_ref, wd_ref, nfin_ref,
                      hf_s, f_s, yf_s, final_norm=final_norm)

    row_chunks = list(range(0, ts, rc))
    n_gate_up = d_ff // FFN_COL_CHUNK
    n_down = d_model // FFN_COL_CHUNK
    pieces = []
    for part in range(ts // FFN_SUB_ROWS):
        pieces += [functools.partial(ffn["gate_up"], c, part) for c in range(n_gate_up)]
        pieces += [functools.partial(ffn["down"], n, part) for n in range(n_down)]
    ffn_matmuls = iter(pieces)

    def ffn_matmul(wanted=True):
        piece = next(ffn_matmuls, None) if wanted else None
        return None if piece is None else piece()

    ffn["norm"]()
    mix["norm"]()
    mix["in_proj"](mix["col_xl"])
    ffn_matmul()
    mix["in_proj"](mix["col_vg"])
    ffn_matmul()
    mix["in_proj"](mix["col_gl"])
    conv_lag = -(-(CONV_LONG - 1) // rc) + 1
    conv_done = {}

    def conv31_chunk(m):
        conv_done[m] = mix["conv31"](row_chunks[m], after=ffn_matmul(m % TIE_EVERY == 0))

    for i, r0 in enumerate(row_chunks):
        mix["conv4_glu"](r0, after=conv_done.get(i - CONV_LEAD))
        if i >= conv_lag - 1:
            conv31_chunk(i - conv_lag + 1)
    for m in range(len(row_chunks) - conv_lag + 1, len(row_chunks)):
        conv31_chunk(m)
    mix["gates"]()
    hprev = [hc_s[:, ls:ls + SLAB] for ls in range(0, d_lru, SLAB)]
    for i, r0 in enumerate(row_chunks):
        hprev = mix["lru"](r0, hprev, after=ffn_matmul())
    hc_s[...] = jnp.concatenate(hprev, axis=1)
    while ffn_matmul() is not None:
        pass
    mix["out_proj"]()
    ffn["finish"]()


def _resident(shape):
    zeros = (0,) * len(shape)
    return pl.BlockSpec(shape, lambda *_: zeros, pipeline_mode=pl.Buffered(1))


def _block_diag(w, block):
    heads, d, _ = w.shape
    per_block = block // d
    eye = jnp.eye(per_block, dtype=w.dtype)
    w = w.reshape(heads // per_block, per_block, d, d)
    return jnp.einsum('nhij,hg->nhigj', w, eye).reshape(heads // per_block, block, block)


def _layer(x, norm_mix, w_in, conv4_w, conv4_b, gate_a_w, gate_a_b, gate_x_w, gate_x_b,
           lru_lambda, dw31_w, dw31_b, cm_ln_g, cm_ln_b, out_norm_lru, out_norm_cm, w_out,
           norm_ffn, w_gate, w_up, w_down, norm_final, final_norm):
    bsz, seq, d_model = x.shape
    d_lru = conv4_b.shape[0]
    d_cm = dw31_b.shape[0]
    d_ff = w_gate.shape[1]
    ts = TOK_TILE
    n_tok = bsz * seq
    n_tiles = n_tok // ts
    assert seq % ts == 0 and ts % ROW_CHUNK == 0 and ts % FFN_SUB_ROWS == 0
    assert d_ff % FFN_COL_CHUNK == 0 and d_model % FFN_COL_CHUNK == 0
    assert d_lru % SLAB == 0 and d_cm % SLAB == 0
    assert conv4_w.shape[0] == CONV_SHORT and dw31_w.shape[0] == CONV_LONG
    bf16 = jnp.bfloat16
    f32 = jnp.float32
    row = lambda v: v.reshape(1, -1)
    assert MXU_DIM % gate_a_w.shape[1] == 0 and d_lru % MXU_DIM == 0
    w_gates = jnp.concatenate([_block_diag(gate_a_w, MXU_DIM), _block_diag(gate_x_w, MXU_DIM)],
                              axis=2).astype(bf16)
    dww_rows = jnp.repeat(dw31_w, SUBLANES, axis=0)
    operands = (
        x.reshape(n_tok, d_model), row(norm_mix), w_in.astype(bf16), conv4_w, row(conv4_b),
        w_gates, row(gate_a_b), row(gate_x_b), row(lru_lambda), dww_rows, row(dw31_b),
        row(cm_ln_g), row(cm_ln_b), row(out_norm_lru), row(out_norm_cm), w_out.astype(bf16),
        row(norm_ffn), w_gate.astype(bf16), w_up.astype(bf16), w_down.astype(bf16),
        row(norm_final))
    in_specs = [pl.BlockSpec((ts, d_model), lambda t: (jnp.minimum(t, n_tiles - 1), 0))]
    in_specs += [_resident(op.shape) for op in operands[1:]]
    scratch = [
        pltpu.VMEM((2, ts, d_model), f32),
        pltpu.VMEM((ts, d_model), bf16),
        pltpu.VMEM((ts, d_ff), bf16),
        pltpu.VMEM((ts, d_model), f32),
        pltpu.VMEM((ts, d_model), bf16),
        pltpu.VMEM((d_lru // LANES, ts + 2 * SUBLANES, LANES), f32),
        pltpu.VMEM((ts, w_in.shape[1]), f32),
        pltpu.VMEM((ts, d_lru), f32),
        pltpu.VMEM((ts, d_lru), bf16),
        pltpu.VMEM((ts, 2 * d_lru), f32),
        pltpu.VMEM((d_cm // LANES, ts + HALO_LONG + EXT_SKEW_ROWS, LANES), f32),
        pltpu.VMEM((ts, d_lru + d_cm), bf16),
        pltpu.VMEM((SUBLANES, d_lru), f32),
    ]
    out = pl.pallas_call(
        functools.partial(_block_kernel, tiles_per_seq=seq // ts, final_norm=final_norm),
        grid=(n_tiles + 1,),
        in_specs=in_specs,
        out_specs=pl.BlockSpec((ts, d_model), lambda t: (jnp.maximum(t - 1, 0), 0)),
        out_shape=jax.ShapeDtypeStruct((n_tok, d_model), x.dtype),
        scratch_shapes=scratch,
        compiler_params=pltpu.CompilerParams(
            dimension_semantics=("arbitrary",),
            vmem_limit_bytes=VMEM_LIMIT_BYTES),
        name="block",
    )(*operands)
    return out.reshape(bsz, seq, d_model)


def kernel(x, norm_mix, w_in, conv4_w, conv4_b, gate_a_w, gate_a_b, gate_x_w, gate_x_b, lru_lambda, dw31_w, dw31_b, cm_ln_g, cm_ln_b, out_norm_lru, out_norm_cm, w_out, norm_ffn, w_gate, w_up, w_down, norm_final):
    depth = w_in.shape[0]
    for l in range(depth):
        x = _layer(x, norm_mix[l], w_in[l], conv4_w[l], conv4_b[l], gate_a_w[l], gate_a_b[l],
                   gate_x_w[l], gate_x_b[l], lru_lambda[l], dw31_w[l], dw31_b[l], cm_ln_g[l],
                   cm_ln_b[l], out_norm_lru[l], out_norm_cm[l], w_out[l], norm_ffn[l], w_gate[l],
                   w_up[l], w_down[l], norm_final, final_norm=(l == depth - 1))
    return x
```

```python
import functools
import math

import jax
import jax.numpy as jnp
from jax import lax
from jax.experimental import pallas as pl
from jax.experimental.pallas import tpu as pltpu

EPS = 1e-6
LRU_C = 8.0
SUBLANES = 8
LANES = 128
SLAB = 512
MXU_DIM = 256
CONV_SHORT = 4
CONV_LONG = 31
HALO_LONG = 32
EXT_SKEW_ROWS = 8
TOK_TILE = 512
FFN_SUB_ROWS = 256
ROW_CHUNK = 16
FFN_COL_CHUNK = 256
V7X_SCOPED_VMEM_BYTES = 60000 * 1024
TIE_EVERY = 3
CONV_LEAD = 8


def _sigmoid(v):
    return 0.5 + 0.5 * jnp.tanh(0.5 * v)


def _silu(v):
    hv = 0.5 * v
    return hv + hv * jnp.tanh(hv)


def _gelu_tanh(v):
    inner = math.sqrt(2.0 / math.pi) * (v + 0.044715 * (v * v * v))
    return v * (0.5 * (1.0 + jnp.tanh(inner)))


def _rms_scale(v, width):
    return lax.rsqrt(jnp.sum(v * v, axis=-1, keepdims=True) * (1.0 / width) + EPS)


def _tie_mask(src, width):
    if src is None:
        return None
    src = jnp.tile(src, (1, width // LANES))
    return jnp.abs(src) < -1.0, src


def _tie(dst, tie):
    if tie is None:
        return dst
    never, src = tie
    dst3 = dst.reshape(-1, SUBLANES, dst.shape[-1])
    return jnp.where(never[None], src[None], dst3).reshape(dst.shape)


def _mixer_pieces(x_ref, x1_ref, nmix_ref, w_in_ref, c4w_ref, c4b_ref, wgate_ref, gab_ref, gxb_ref,
                  lam_ref, dww_ref, dwb_ref, lng_ref, lnb_ref, onl_ref, onc_ref, w_out_ref,
                  h_s, xl_s, proj_s, c_s, cb_s, gates_s, ext_s, y_s):
    ts, d_model = x_ref.shape
    d_lru = c4b_ref.shape[1]
    d_cm = dwb_ref.shape[1]
    rc = ROW_CHUNK
    groups = rc // SUBLANES
    first_tap = HALO_LONG - (CONV_LONG - 1)
    col_xl = slice(0, d_lru)
    col_gl = slice(d_lru, 2 * d_lru)
    col_vg = slice(2 * d_lru, 2 * d_lru + 2 * d_cm)

    def norm():
        nmix = nmix_ref[...]
        for r0 in range(0, ts, rc):
            xv = x_ref[pl.ds(r0, rc), :]
            h_s[pl.ds(r0, rc), :] = (xv * _rms_scale(xv, d_model) * nmix).astype(jnp.bfloat16)

    def in_proj(cols):
        res = jnp.dot(h_s[...], w_in_ref[:, cols], preferred_element_type=jnp.float32)
        if cols == col_xl:
            for ls in range(0, d_lru, LANES):
                xl_s[ls // LANES, pl.ds(SUBLANES, ts), :] = res[:, ls:ls + LANES]
        else:
            proj_s[:, cols] = res

    def slab_cols(width):
        return [slice(ls, ls + SLAB) for ls in range(0, width, SLAB)]

    def conv4_glu(r0, after=None):
        tie = _tie_mask(after, SLAB)
        for cols in slab_cols(d_lru):
            c4w = c4w_ref[:, cols]
            acc = _tie(jnp.broadcast_to(c4b_ref[:, cols], (SUBLANES, SLAB)), tie)[0:1, :]
            for s in range(CONV_SHORT):
                tap = jnp.concatenate(
                    [xl_s[(cols.start + ls) // LANES, pl.ds(r0 + SUBLANES - s, rc), :]
                     for ls in range(0, SLAB, LANES)], axis=1)
                acc = acc + c4w[CONV_SHORT - 1 - s:CONV_SHORT - s, :] * tap
            c_s[pl.ds(r0, rc), cols] = acc
            cb_s[pl.ds(r0, rc), cols] = acc.astype(jnp.bfloat16)
        for cols in slab_cols(d_cm):
            v = proj_s[pl.ds(r0, rc), pl.ds(2 * d_lru + cols.start, SLAB)]
            g = proj_s[pl.ds(r0, rc), pl.ds(2 * d_lru + d_cm + cols.start, SLAB)]
            glu = v * _sigmoid(g)
            for ls in range(0, SLAB, LANES):
                ext_s[(cols.start + ls) // LANES, pl.ds(r0 + HALO_LONG, rc), :] = (
                    glu[:, ls:ls + LANES])

    def gates():
        blk = wgate_ref.shape[1]
        for b in range(wgate_ref.shape[0]):
            res = jnp.dot(cb_s[:, b * blk:(b + 1) * blk], wgate_ref[b],
                          preferred_element_type=jnp.float32)
            gates_s[:, b * blk:(b + 1) * blk] = res[:, :blk]
            gates_s[:, d_lru + b * blk:d_lru + (b + 1) * blk] = res[:, blk:]

    def lru(r0, hprev, after=None):
        row_in_group = lax.broadcasted_iota(jnp.int32, (groups, SUBLANES, SLAB), 1)
        tie = _tie_mask(after, SLAB)
        hprev = [_tie(hp, tie) for hp in hprev]
        ys, hnext = [], []
        for cols, hp in zip(slab_cols(d_lru), hprev):
            z = -lam_ref[:, cols]
            neg_c_softplus = -LRU_C * (jnp.maximum(z, 0.0) + jnp.log1p(jnp.exp(-jnp.abs(z))))
            cv = c_s[pl.ds(r0, rc), cols]
            r_gate = _sigmoid(gates_s[pl.ds(r0, rc), cols] + gab_ref[:, cols])
            i_gate = _sigmoid(gates_s[pl.ds(r0, rc), pl.ds(d_lru + cols.start, SLAB)]
                              + gxb_ref[:, cols])
            log_a = neg_c_softplus * r_gate
            a = jnp.exp(log_a)
            b = jnp.sqrt(-jnp.tanh(log_a) * (a * a + 1.0)) * (i_gate * cv)
            a3 = a.reshape(groups, SUBLANES, SLAB)
            b3 = b.reshape(groups, SUBLANES, SLAB)
            for s in (1, 2, 4):
                keep = row_in_group >= s
                a_prev = jnp.where(keep, pltpu.roll(a3, s, axis=1), 1.0)
                b_prev = jnp.where(keep, pltpu.roll(b3, s, axis=1), 0.0)
                b3 = a3 * b_prev + b3
                a3 = a3 * a_prev
            hs = []
            for gi in range(groups):
                hg = b3[gi] + a3[gi] * hp
                hs.append(hg)
                hp = jnp.broadcast_to(hg[SUBLANES - 1:SUBLANES, :], (SUBLANES, SLAB))
            hnext.append(hp)
            gl = proj_s[pl.ds(r0, rc), pl.ds(d_lru + cols.start, SLAB)]
            ys.append(jnp.concatenate(hs, axis=0) * _gelu_tanh(gl))
        scale = lax.rsqrt(sum(jnp.sum(y * y, axis=-1, keepdims=True) for y in ys)
                          * (1.0 / d_lru) + EPS)
        for cols, y in zip(slab_cols(d_lru), ys):
            y_s[pl.ds(r0, rc), cols] = (y * scale * onl_ref[:, cols]).astype(jnp.bfloat16)
        return hnext

    def conv31(r0, after=None):
        ds = []
        tie = _tie_mask(after, SLAB)
        for cols in slab_cols(d_cm):
            acc = jnp.broadcast_to(dwb_ref[:, cols], (rc, SLAB)).reshape(groups, SUBLANES, SLAB)
            for k in range(CONV_LONG):
                tap = jnp.concatenate(
                    [ext_s[(cols.start + ls) // LANES, pl.ds(r0 + first_tap + k, rc), :]
                     for ls in range(0, SLAB, LANES)], axis=1).reshape(groups, SUBLANES, SLAB)
                wk = _tie(dww_ref[k * SUBLANES:(k + 1) * SUBLANES, cols], tie)
                acc = acc + wk[None] * tap
            ds.append(acc.reshape(rc, SLAB))
        mu = sum(jnp.sum(d, axis=-1, keepdims=True) for d in ds) * (1.0 / d_cm)
        dcs = [d - mu for d in ds]
        var = sum(jnp.sum(dc * dc, axis=-1, keepdims=True) for dc in dcs) * (1.0 / d_cm)
        inv = lax.rsqrt(var + EPS)
        ys = []
        for cols, dc in zip(slab_cols(d_cm), dcs):
            ln = dc * inv * lng_ref[:, cols] + lnb_ref[:, cols]
            ys.append(ln * _sigmoid(ln))
        scale = lax.rsqrt(sum(jnp.sum(y * y, axis=-1, keepdims=True) for y in ys)
                          * (1.0 / d_cm) + EPS)
        for cols, y in zip(slab_cols(d_cm), ys):
            y_s[pl.ds(r0, rc), pl.ds(d_lru + cols.start, SLAB)] = (
                y * scale * onc_ref[:, cols]).astype(jnp.bfloat16)
        return ys[0][0:SUBLANES, 0:LANES]

    def out_proj():
        x1_ref[...] = x_ref[...] + jnp.dot(y_s[...], w_out_ref[...],
                                           preferred_element_type=jnp.float32)
        xl_s[:, 0:SUBLANES, :] = xl_s[:, ts:ts + SUBLANES, :]
        ext_s[:, 0:HALO_LONG, :] = ext_s[:, ts:ts + HALO_LONG, :]

    return dict(norm=norm, in_proj=in_proj, conv4_glu=conv4_glu, gates=gates,
                lru=lru, conv31=conv31, out_proj=out_proj,
                col_xl=col_xl, col_gl=col_gl, col_vg=col_vg)


def _ffn_pieces(x_ref, o_ref, nffn_ref, wg_ref, wu_ref, wd_ref, nfin_ref, hf_s, f_s, yf_s,
                *, final_norm):
    ts, d_model = x_ref.shape

    def norm():
        xv = x_ref[...]
        hf_s[...] = (xv * _rms_scale(xv, d_model) * nffn_ref[...]).astype(jnp.bfloat16)

    def gate_up(c, part):
        rows = slice(part * FFN_SUB_ROWS, (part + 1) * FFN_SUB_ROWS)
        cols = slice(c * FFN_COL_CHUNK, (c + 1) * FFN_COL_CHUNK)
        h = hf_s[rows, :]
        g = jnp.dot(h, wg_ref[:, cols], preferred_element_type=jnp.float32)
        u = jnp.dot(h, wu_ref[:, cols], preferred_element_type=jnp.float32)
        f = _silu(g) * u
        f_s[rows, cols] = f.astype(jnp.bfloat16)
        return f[-SUBLANES:, -LANES:]

    def down(n, part):
        rows = slice(part * FFN_SUB_ROWS, (part + 1) * FFN_SUB_ROWS)
        cols = slice(n * FFN_COL_CHUNK, (n + 1) * FFN_COL_CHUNK)
        y = x_ref[rows, cols] + jnp.dot(f_s[rows, :], wd_ref[:, cols],
                                        preferred_element_type=jnp.float32)
        yf_s[rows, cols] = y
        return y[-SUBLANES:, -LANES:]

    def finish():
        y = yf_s[...]
        if final_norm:
            y = y * _rms_scale(y, d_model) * nfin_ref[...]
        o_ref[...] = y

    return dict(norm=norm, gate_up=gate_up, down=down, finish=finish)


def _emit_step(refs, slot, *, do_mixer, do_ffn, final_norm):
    (x_ref, nmix_ref, w_in_ref, c4w_ref, c4b_ref, wgate_ref, gab_ref, gxb_ref, lam_ref, dww_ref,
     dwb_ref, lng_ref, lnb_ref, onl_ref, onc_ref, w_out_ref, nffn_ref, wg_ref, wu_ref, wd_ref,
     nfin_ref, o_ref, x1_s, hf_s, f_s, yf_s, h_s, xl_s, proj_s, c_s, cb_s, gates_s, ext_s, y_s,
     hc_s) = refs
    ts, d_model = x_ref.shape
    d_lru = c4b_ref.shape[1]
    d_ff = wd_ref.shape[0]
    rc = ROW_CHUNK

    mix = _mixer_pieces(x_ref, x1_s.at[slot], nmix_ref, w_in_ref, c4w_ref, c4b_ref, wgate_ref,
                        gab_ref, gxb_ref, lam_ref, dww_ref, dwb_ref, lng_ref, lnb_ref, onl_ref,
                        onc_ref, w_out_ref, h_s, xl_s, proj_s, c_s, cb_s, gates_s, ext_s, y_s)
    ffn = _ffn_pieces(x1_s.at[1 - slot], o_ref, nffn_ref, wg_ref, wu_ref, wd_ref, nfin_ref,
                      hf_s, f_s, yf_s, final_norm=final_norm)

    row_chunks = list(range(0, ts, rc))
    n_gate_up = d_ff // FFN_COL_CHUNK
    n_down = d_model // FFN_COL_CHUNK
    pieces = []
    if do_ffn:
        for part in range(ts // FFN_SUB_ROWS):
            pieces += [functools.partial(ffn["gate_up"], c, part) for c in range(n_gate_up)]
            pieces += [functools.partial(ffn["down"], n, part) for n in range(n_down)]
    ffn_matmuls = iter(pieces)

    def ffn_matmul(wanted=True):
        piece = next(ffn_matmuls, None) if wanted else None
        return None if piece is None else piece()

    if do_ffn:
        ffn["norm"]()
    if do_mixer:
        mix["norm"]()
        mix["in_proj"](mix["col_xl"])
        ffn_matmul()
        mix["in_proj"](mix["col_vg"])
        ffn_matmul()
        mix["in_proj"](mix["col_gl"])
        conv_lag = -(-(CONV_LONG - 1) // rc) + 1
        conv_done = {}

        def conv31_chunk(m):
            conv_done[m] = mix["conv31"](row_chunks[m], after=ffn_matmul(m % TIE_EVERY == 0))

        for i, r0 in enumerate(row_chunks):
            mix["conv4_glu"](r0, after=conv_done.get(i - CONV_LEAD))
            if i >= conv_lag - 1:
                conv31_chunk(i - conv_lag + 1)
        for m in range(len(row_chunks) - conv_lag + 1, len(row_chunks)):
            conv31_chunk(m)
        mix["gates"]()
        hprev = [hc_s[:, ls:ls + SLAB] for ls in range(0, d_lru, SLAB)]
        for i, r0 in enumerate(row_chunks):
            hprev = mix["lru"](r0, hprev, after=ffn_matmul())
        hc_s[...] = jnp.concatenate(hprev, axis=1)
    while ffn_matmul() is not None:
        pass
    if do_mixer:
        mix["out_proj"]()
    if do_ffn:
        ffn["finish"]()


def _block_kernel(x_ref, nmix_ref, w_in_ref, c4w_ref, c4b_ref, wgate_ref, gab_ref, gxb_ref,
                  lam_ref, dww_ref, dwb_ref, lng_ref, lnb_ref, onl_ref, onc_ref, w_out_ref,
                  nffn_ref, wg_ref, wu_ref, wd_ref, nfin_ref,
                  o_ref,
                  x1_s, hf_s, f_s, yf_s, h_s, xl_s, proj_s, c_s, cb_s, gates_s, ext_s, y_s, hc_s,
                  *, n_tiles, tiles_per_seq, final_norm):
    refs = (x_ref, nmix_ref, w_in_ref, c4w_ref, c4b_ref, wgate_ref, gab_ref, gxb_ref, lam_ref,
            dww_ref, dwb_ref, lng_ref, lnb_ref, onl_ref, onc_ref, w_out_ref, nffn_ref, wg_ref,
            wu_ref, wd_ref, nfin_ref, o_ref, x1_s, hf_s, f_s, yf_s, h_s, xl_s, proj_s, c_s, cb_s,
            gates_s, ext_s, y_s, hc_s)
    t = pl.program_id(0)
    slot = lax.rem(t, 2)

    @pl.when(lax.rem(t, tiles_per_seq) == 0)
    def _():
        xl_s[:, 0:SUBLANES, :] = jnp.zeros((xl_s.shape[0], SUBLANES, LANES), jnp.float32)
        ext_s[:, 0:HALO_LONG, :] = jnp.zeros((ext_s.shape[0], HALO_LONG, LANES), jnp.float32)
        hc_s[...] = jnp.zeros_like(hc_s)

    emit = functools.partial(_emit_step, refs, slot, final_norm=final_norm)
    pl.when(t == 0)(functools.partial(emit, do_mixer=True, do_ffn=False))
    pl.when(jnp.logical_and(t > 0, t < n_tiles))(
        functools.partial(emit, do_mixer=True, do_ffn=True))
    pl.when(t == n_tiles)(functools.partial(emit, do_mixer=False, do_ffn=True))


def _resident(shape):
    zeros = (0,) * len(shape)
    return pl.BlockSpec(shape, lambda *_: zeros, pipeline_mode=pl.Buffered(1))


def _block_diag(w, block):
    heads, d, _ = w.shape
    per_block = block // d
    eye = jnp.eye(per_block, dtype=w.dtype)
    w = w.reshape(heads // per_block, per_block, d, d)
    return jnp.einsum('nhij,hg->nhigj', w, eye).reshape(heads // per_block, block, block)


def _layer(x, norm_mix, w_in, conv4_w, conv4_b, gate_a_w, gate_a_b, gate_x_w, gate_x_b,
           lru_lambda, dw31_w, dw31_b, cm_ln_g, cm_ln_b, out_norm_lru, out_norm_cm, w_out,
           norm_ffn, w_gate, w_up, w_down, norm_final, final_norm):
    bsz, seq, d_model = x.shape
    d_lru = conv4_b.shape[0]
    d_cm = dw31_b.shape[0]
    d_ff = w_gate.shape[1]
    ts = TOK_TILE
    n_tok = bsz * seq
    n_tiles = n_tok // ts
    assert seq % ts == 0 and ts % ROW_CHUNK == 0 and ts % FFN_SUB_ROWS == 0
    assert d_ff % FFN_COL_CHUNK == 0 and d_model % FFN_COL_CHUNK == 0
    assert d_lru % SLAB == 0 and d_cm % SLAB == 0
    assert conv4_w.shape[0] == CONV_SHORT and dw31_w.shape[0] == CONV_LONG
    bf16 = jnp.bfloat16
    f32 = jnp.float32
    row = lambda v: v.reshape(1, -1)
    assert MXU_DIM % gate_a_w.shape[1] == 0 and d_lru % MXU_DIM == 0
    w_gates = jnp.concatenate([_block_diag(gate_a_w, MXU_DIM), _block_diag(gate_x_w, MXU_DIM)],
                              axis=2).astype(bf16)
    dww_rows = jnp.repeat(dw31_w, SUBLANES, axis=0)
    operands = (
        x.reshape(n_tok, d_model), row(norm_mix), w_in.astype(bf16), conv4_w, row(conv4_b),
        w_gates, row(gate_a_b), row(gate_x_b), row(lru_lambda), dww_rows, row(dw31_b),
        row(cm_ln_g), row(cm_ln_b), row(out_norm_lru), row(out_norm_cm), w_out.astype(bf16),
        row(norm_ffn), w_gate.astype(bf16), w_up.astype(bf16), w_down.astype(bf16),
        row(norm_final))
    in_specs = [pl.BlockSpec((ts, d_model), lambda t: (jnp.minimum(t, n_tiles - 1), 0))]
    in_specs += [_resident(op.shape) for op in operands[1:]]
    scratch_shapes = [
        ((2, ts, d_model), f32),
        ((ts, d_model), bf16),
        ((ts, d_ff), bf16),
        ((ts, d_model), f32),
        ((ts, d_model), bf16),
        ((d_lru // LANES, ts + 2 * SUBLANES, LANES), f32),
        ((ts, w_in.shape[1]), f32),
        ((ts, d_lru), f32),
        ((ts, d_lru), bf16),
        ((ts, 2 * d_lru), f32),
        ((d_cm // LANES, ts + HALO_LONG + EXT_SKEW_ROWS, LANES), f32),
        ((ts, d_lru + d_cm), bf16),
        ((SUBLANES, d_lru), f32),
    ]
    scratch = [pltpu.VMEM(shape, dtype) for shape, dtype in scratch_shapes]
    nbytes = lambda shape, dtype: math.prod(shape) * jnp.dtype(dtype).itemsize
    vmem_bytes = (sum(nbytes(op.shape, op.dtype) for op in operands[1:])
                  + 2 * 2 * nbytes((ts, d_model), x.dtype)
                  + sum(nbytes(shape, dtype) for shape, dtype in scratch_shapes))
    out = pl.pallas_call(
        functools.partial(_block_kernel, n_tiles=n_tiles, tiles_per_seq=seq // ts,
                          final_norm=final_norm),
        grid=(n_tiles + 1,),
        in_specs=in_specs,
        out_specs=pl.BlockSpec((ts, d_model), lambda t: (jnp.maximum(t - 1, 0), 0)),
        out_shape=jax.ShapeDtypeStruct((n_tok, d_model), x.dtype),
        scratch_shapes=scratch,
        compiler_params=pltpu.CompilerParams(
            dimension_semantics=("arbitrary",),
            vmem_limit_bytes=min(vmem_bytes + vmem_bytes // 8, V7X_SCOPED_VMEM_BYTES)),
        name="block",
    )(*operands)
    return out.reshape(bsz, seq, d_model)


def kernel(x, norm_mix, w_in, conv4_w, conv4_b, gate_a_w, gate_a_b, gate_x_w, gate_x_b, lru_lambda, dw31_w, dw31_b, cm_ln_g, cm_ln_b, out_norm_lru, out_norm_cm, w_out, norm_ffn, w_gate, w_up, w_down, norm_final):
    depth = w_in.shape[0]
    for l in range(depth):
        x = _layer(x, norm_mix[l], w_in[l], conv4_w[l], conv4_b[l], gate_a_w[l], gate_a_b[l],
                   gate_x_w[l], gate_x_b[l], lru_lambda[l], dw31_w[l], dw31_b[l], cm_ln_g[l],
                   cm_ln_b[l], out_norm_lru[l], out_norm_cm[l], w_out[l], norm_ffn[l], w_gate[l],
                   w_up[l], w_down[l], norm_final, final_norm=(l == depth - 1))
    return x
```
